```python
import math
import jax, jax.numpy as jnp
from jax import lax
import numpy as np

D_MODEL = 1024
BATCH = 1
SEQ = 16384
DEPTH = 1
DEC_BATCH = 32
DEC_SEQ = 8
PAST_LEN = 16384
PAGE_SIZE = 128

N_META = 16
MIX_W = D_MODEL
GDN_DK = 128
GDN_DV = 128
GDN_HEADS = (MIX_W // 2) // GDN_DV
GDN_CHUNK = 64
CONV_W = 4
SB_HEAD_DIM = 64
SB_HEADS = (MIX_W // 2) // SB_HEAD_DIM
SB_BLOCK = 128
SB_BIAS_HI = -4.0
SB_BIAS_LO = -11.0
GDN_QK = GDN_HEADS * GDN_DK
GDN_VW = GDN_HEADS * GDN_DV
CONV_DIM = 2 * GDN_QK + GDN_VW
SB_W = SB_HEADS * SB_HEAD_DIM
PROJ_SIZES = (CONV_DIM, GDN_VW, GDN_HEADS, GDN_HEADS, SB_W, SB_W, SB_W, SB_W)
IN_DIM = CONV_DIM + GDN_VW + 2 * GDN_HEADS + 4 * SB_W
ALPHA = (2 * DEPTH) ** 0.25
BETA_INIT = (8 * DEPTH) ** -0.25
LN_EPS = 1e-5
RMS_EPS = 1e-6
L2_EPS = 1e-6

kernel_name = "hymba_gdn_stickbreak_step"


def _layernorm(x, g, b):
    xf = x.astype(jnp.float32)
    mu = jnp.mean(xf, -1, keepdims=True)
    var = jnp.mean(jnp.square(xf - mu), -1, keepdims=True)
    return (xf - mu) * lax.rsqrt(var + LN_EPS) * g.astype(jnp.float32) + b.astype(jnp.float32)


def _rmsnorm(x, g):
    return x * lax.rsqrt(jnp.mean(x * x, -1, keepdims=True) + RMS_EPS) * g.astype(jnp.float32)


def _l2norm(x):
    return x * lax.rsqrt(jnp.sum(x * x, -1, keepdims=True) + L2_EPS)


def _project(h, w_in):
    p = jnp.einsum('bld,de->ble', h, w_in).astype(jnp.float32)
    splits = [int(s) for s in np.cumsum(PROJ_SIZES)[:-1]]
    return jnp.split(p, splits, axis=-1)


def _causal_conv(x, buf, w):
    L = x.shape[1]
    xx = jnp.concatenate([buf, x], axis=1)
    wf = w.astype(jnp.float32)
    y = sum(xx[:, i:i + L] * wf[i] for i in range(CONV_W))
    return jax.nn.silu(y), xx[:, -(CONV_W - 1):]


def _gdn_qkv(c):
    B, L, _ = c.shape
    q, k, v = jnp.split(c, [GDN_QK, 2 * GDN_QK], axis=-1)
    q = _l2norm(q.reshape(B, L, GDN_HEADS, GDN_DK)) * (GDN_DK ** -0.5)
    k = _l2norm(k.reshape(B, L, GDN_HEADS, GDN_DK))
    v = v.reshape(B, L, GDN_HEADS, GDN_DV)
    return q, k, v


def _gdn_gates(a, b, a_log, dt_bias):
    g = -jnp.exp(a_log.astype(jnp.float32)) * jax.nn.softplus(a + dt_bias.astype(jnp.float32))
    return g, jax.nn.sigmoid(b)


def _gdn_chunked(q, k, v, g, beta, s0, chunk):
    B, L, H, DK = q.shape
    DV = v.shape[-1]
    n = L // chunk

    def blocks(t):
        return jnp.swapaxes(t.reshape((B, n, chunk) + t.shape[2:]), 2, 3)

    q, k, v, beta = blocks(q), blocks(k), blocks(v), blocks(beta)
    g = jnp.cumsum(blocks(g), axis=-1)
    idx = jnp.arange(chunk)
    causal = idx[:, None] >= idx[None, :]
    strict = idx[:, None] > idx[None, :]
    decay = jnp.exp(jnp.where(causal, g[..., :, None] - g[..., None, :], -jnp.inf))
    kb = k * beta[..., None]
    m = jnp.where(strict, jnp.einsum('bnhid,bnhjd->bnhij', kb, k) * decay, 0.0)
    rhs = jnp.concatenate([v * beta[..., None], kb * jnp.exp(g)[..., None]], axis=-1)
    sol = lax.linalg.triangular_solve(m + jnp.eye(chunk, dtype=m.dtype), rhs,
                                      left_side=True, lower=True, unit_diagonal=True)
    u, w = sol[..., :DV], sol[..., DV:]
    qk = jnp.einsum('bnhid,bnhjd->bnhij', q, k) * decay
    qd = q * jnp.exp(g)[..., None]
    gl = g[..., -1]
    kd = k * jnp.exp(gl[..., None] - g)[..., None]

    def step(s, xs):
        u_c, w_c, qk_c, qd_c, kd_c, gl_c = xs
        v_new = u_c - jnp.einsum('bhck,bhkv->bhcv', w_c, s)
        o = jnp.einsum('bhck,bhkv->bhcv', qd_c, s) + jnp.einsum('bhij,bhjv->bhiv', qk_c, v_new)
        s = s * jnp.exp(gl_c)[..., None, None] + jnp.einsum('bhck,bhcv->bhkv', kd_c, v_new)
        return s, o

    xs = tuple(jnp.moveaxis(t, 1, 0) for t in (u, w, qk, qd, kd, gl))
    s, o = lax.scan(step, s0, xs)
    return o.transpose(1, 0, 3, 2, 4).reshape(B, L, H, DV), s


def _sb_block(q, k, v, q_pos, k_pos, bias):
    z = jnp.einsum('bqhd,bkhd->bhqk', q.astype(jnp.float32), k.astype(jnp.float32)) * (SB_HEAD_DIM ** -0.5)
    z = z + bias.astype(jnp.float32)[None, :, None, None]
    vis = k_pos[None, :] < q_pos[:, None]
    log_keep = jnp.where(vis, jax.nn.log_sigmoid(-z), 0.0)
    after = lax.cumsum(log_keep, axis=3, reverse=True) - log_keep
    a = jnp.where(vis, jnp.exp(jax.nn.log_sigmoid(z) + after), 0.0)
    return jnp.einsum('bhqk,bkhd->bqhd', a, v.astype(jnp.float32))


def _merge(h, o_gdn, z_gdn, o_sb, z_sb, gdn_norm_g, w_out, ln_g, ln_b):
    B, L, _ = h.shape
    og = _rmsnorm(o_gdn, gdn_norm_g).reshape(B, L, GDN_VW) * jax.nn.silu(z_gdn)
    osb = o_sb.reshape(B, L, SB_W) * jax.nn.silu(z_sb)
    mix = jnp.einsum('ble,ed->bld', jnp.concatenate([og, osb], axis=-1), w_out.astype(jnp.float32))
    return _layernorm(ALPHA * h.astype(jnp.float32) + mix, ln_g, ln_b).astype(h.dtype)


def _prompt_layer(h, w_in, conv_w, a_log, dt_bias, gdn_norm_g, sb_bias, w_out, ln_g, ln_b):
    B, L, _ = h.shape
    c, zg, a, b, q, k, v, zs = _project(h, w_in)
    c, conv_buf = _causal_conv(c, jnp.zeros((B, CONV_W - 1, CONV_DIM), jnp.float32), conv_w)
    gq, gk, gv = _gdn_qkv(c)
    g, beta = _gdn_gates(a, b, a_log, dt_bias)
    m = N_META
    s0 = jnp.zeros((B, GDN_HEADS, GDN_DK, GDN_DV), jnp.float32)
    o_meta, s1 = _gdn_chunked(gq[:, :m], gk[:, :m], gv[:, :m], g[:, :m], beta[:, :m], s0, m)
    o_real, s2 = _gdn_chunked(gq[:, m:], gk[:, m:], gv[:, m:], g[:, m:], beta[:, m:], s1, GDN_CHUNK)
    o_gdn = jnp.concatenate([o_meta, o_real], axis=1)
    q = q.reshape(B, L, SB_HEADS, SB_HEAD_DIM)
    k = k.reshape(B, L, SB_HEADS, SB_HEAD_DIM)
    v = v.reshape(B, L, SB_HEADS, SB_HEAD_DIM)
    pos = jnp.arange(L, dtype=jnp.int32)
    o_sb_meta = _sb_block(q[:, :m], k[:, :m], v[:, :m], pos[:m], pos[:m], sb_bias)
    n_blk = (L - m) // SB_BLOCK
    qb = jnp.moveaxis(q[:, m:].reshape(B, n_blk, SB_BLOCK, SB_HEADS, SB_HEAD_DIM), 1, 0)
    pb = pos[m:].reshape(n_blk, SB_BLOCK)
    ob = lax.map(lambda t: _sb_block(t[0], k, v, t[1], pos, sb_bias), (qb, pb))
    o_sb_real = jnp.moveaxis(ob, 0, 1).reshape(B, L - m, SB_HEADS, SB_HEAD_DIM)
    o_sb = jnp.concatenate([o_sb_meta, o_sb_real], axis=1)
    y = _merge(h, o_gdn, zg, o_sb, zs, gdn_norm_g, w_out, ln_g, ln_b)
    return y, k.astype(h.dtype), v.astype(h.dtype), s2, conv_buf.astype(h.dtype)


def _sample_layer(h, cache_k, cache_v, page_table, state_gdn, state_conv,
                  w_in, conv_w, a_log, dt_bias, gdn_norm_g, sb_bias, w_out, ln_g, ln_b):
    B, T, _ = h.shape
    c, zg, a, b, q, k, v, zs = _project(h, w_in)
    c, conv_buf = _causal_conv(c, state_conv.astype(jnp.float32), conv_w)
    gq, gk, gv = _gdn_qkv(c)
    g, beta = _gdn_gates(a, b, a_log, dt_bias)
    o_gdn, s_new = _gdn_chunked(gq, gk, gv, g, beta, state_gdn.astype(jnp.float32), T)
    q = q.reshape(B, T, SB_HEADS, SB_HEAD_DIM)
    k = k.reshape(B, T, SB_HEADS, SB_HEAD_DIM)
    v = v.reshape(B, T, SB_HEADS, SB_HEAD_DIM)
    past = page_table.shape[1] * cache_k.shape[1]
    k_pos = jnp.arange(past + T, dtype=jnp.int32)
    q_pos = k_pos[past:]

    def one_seq(t):
        pt, q1, k1, v1 = t
        kk = jnp.concatenate([cache_k[pt].reshape(past, SB_HEADS, SB_HEAD_DIM).astype(jnp.float32), k1], axis=0)
        vv = jnp.concatenate([cache_v[pt].reshape(past, SB_HEADS, SB_HEAD_DIM).astype(jnp.float32), v1], axis=0)
        return _sb_block(q1[None], kk[None], vv[None], q_pos, k_pos, sb_bias)[0]

    o_sb = lax.map(one_seq, (page_table, q, k, v))
    y = _merge(h, o_gdn, zg, o_sb, zs, gdn_norm_g, w_out, ln_g, ln_b)
    return y, k.astype(h.dtype), v.astype(h.dtype), s_new, conv_buf.astype(h.dtype)


def setup_inputs(seed: int = 0) -> dict:
    key = jax.random.key(seed)
    ks = jax.random.split(key, 20)
    nrm = jax.random.normal
    n_pages = PAST_LEN // PAGE_SIZE
    n_phys = (DEC_BATCH * n_pages * 5) // 4
    x_prompt = nrm(ks[0], (BATCH, SEQ, D_MODEL), jnp.float32)
    x_sample = nrm(ks[1], (DEC_BATCH, DEC_SEQ, D_MODEL), jnp.float32)
    cache_k = nrm(ks[2], (DEPTH, n_phys, PAGE_SIZE, SB_HEADS, SB_HEAD_DIM), jnp.float32)
    cache_v = nrm(ks[3], (DEPTH, n_phys, PAGE_SIZE, SB_HEADS, SB_HEAD_DIM), jnp.float32)
    page_table = jax.random.permutation(ks[4], n_phys)[:DEC_BATCH * n_pages].reshape(DEC_BATCH, n_pages).astype(jnp.int32)
    state_gdn = 0.1 * nrm(ks[5], (DEPTH, DEC_BATCH, GDN_HEADS, GDN_DK, GDN_DV), jnp.float32)
    state_conv = nrm(ks[6], (DEPTH, DEC_BATCH, CONV_W - 1, CONV_DIM), jnp.float32)
    meta_tokens = nrm(ks[7], (N_META, D_MODEL), jnp.float32)
    ln_in_g = 1.0 + 0.02 * nrm(ks[8], (D_MODEL,), jnp.float32)
    ln_in_b = 0.02 * nrm(ks[9], (D_MODEL,), jnp.float32)
    w_in = nrm(ks[10], (DEPTH, D_MODEL, IN_DIM), jnp.float32) * (D_MODEL ** -0.5)
    conv_w = nrm(ks[11], (DEPTH, CONV_W, CONV_DIM), jnp.float32) * (CONV_W ** -0.5)
    a_log = jnp.log(jax.random.uniform(ks[12], (DEPTH, GDN_HEADS), jnp.float32, 1.0, 16.0))
    dt = jnp.exp(jax.random.uniform(ks[13], (DEPTH, GDN_HEADS), jnp.float32, math.log(1e-3), math.log(1e-1)))
    dt_bias = dt + jnp.log(-jnp.expm1(-dt))
    gdn_norm_g = 1.0 + 0.02 * nrm(ks[14], (DEPTH, GDN_DV), jnp.float32)
    sb_bias = jnp.broadcast_to(jnp.linspace(SB_BIAS_HI, SB_BIAS_LO, SB_HEADS, dtype=jnp.float32), (DEPTH, SB_HEADS)) \
        + 0.1 * nrm(ks[18], (DEPTH, SB_HEADS), jnp.float32)
    w_out = nrm(ks[15], (DEPTH, MIX_W, D_MODEL), jnp.float32) * (MIX_W ** -0.5) * BETA_INIT
    ln_out_g = 1.0 + 0.02 * nrm(ks[16], (DEPTH, D_MODEL), jnp.float32)
    ln_out_b = 0.02 * nrm(ks[17], (DEPTH, D_MODEL), jnp.float32)
    return {"x_prompt": x_prompt, "x_sample": x_sample, "cache_k": cache_k, "cache_v": cache_v,
            "page_table": page_table, "state_gdn": state_gdn, "state_conv": state_conv,
            "meta_tokens": meta_tokens, "ln_in_g": ln_in_g, "ln_in_b": ln_in_b, "w_in": w_in,
            "conv_w": conv_w, "a_log": a_log, "dt_bias": dt_bias, "gdn_norm_g": gdn_norm_g,
            "sb_bias": sb_bias, "w_out": w_out, "ln_out_g": ln_out_g, "ln_out_b": ln_out_b}


def reference(x_prompt, x_sample, cache_k, cache_v, page_table, state_gdn, state_conv,
              meta_tokens, ln_in_g, ln_in_b, w_in, conv_w, a_log, dt_bias, gdn_norm_g,
              sb_bias, w_out, ln_out_g, ln_out_b):
    B = x_prompt.shape[0]
    meta = jnp.broadcast_to(meta_tokens[None].astype(x_prompt.dtype), (B, N_META, D_MODEL))
    hp = _layernorm(jnp.concatenate([meta, x_prompt], axis=1), ln_in_g, ln_in_b).astype(x_prompt.dtype)
    hs = _layernorm(x_sample, ln_in_g, ln_in_b).astype(x_sample.dtype)
    kp, vp, ksm, vsm, sp, ss, cp, cs = [], [], [], [], [], [], [], []
    for l in range(DEPTH):
        hp, k1, v1, s1, c1 = _prompt_layer(hp, w_in[l], conv_w[l], a_log[l], dt_bias[l], gdn_norm_g[l],
                                           sb_bias[l], w_out[l], ln_out_g[l], ln_out_b[l])
        hs, k2, v2, s2, c2 = _sample_layer(hs, cache_k[l], cache_v[l], page_table, state_gdn[l], state_conv[l],
                                           w_in[l], conv_w[l], a_log[l], dt_bias[l], gdn_norm_g[l],
                                           sb_bias[l], w_out[l], ln_out_g[l], ln_out_b[l])
        kp.append(k1); vp.append(v1); sp.append(s1); cp.append(c1)
        ksm.append(k2); vsm.append(v2); ss.append(s2); cs.append(c2)
    y_prompt = hp[:, N_META:]
    y_sample = hs
    return (y_prompt, y_sample, jnp.stack(kp), jnp.stack(vp), jnp.stack(ksm), jnp.stack(vsm),
            jnp.stack(sp), jnp.stack(ss), jnp.stack(cp), jnp.stack(cs))
```

```python
import functools
import math

import jax
import jax.numpy as jnp
from jax import lax
from jax.experimental import pallas as pl
from jax.experimental.pallas import tpu as pltpu

F32 = jnp.float32
BF16 = jnp.bfloat16
HIGHEST = lax.Precision.HIGHEST

D_MODEL = 1024
N_META = 16
GDN_HEADS = 4
GDN_DK = 128
GDN_DV = 128
GDN_CHUNK = 64
CONV_W = 4
SB_HEADS = 8
SB_HEAD_DIM = 64
GDN_QK = GDN_HEADS * GDN_DK
GDN_VW = GDN_HEADS * GDN_DV
CONV_DIM = 2 * GDN_QK + GDN_VW
SB_W = SB_HEADS * SB_HEAD_DIM
PROJ_SIZES = (CONV_DIM, GDN_VW, GDN_HEADS, GDN_HEADS, SB_W, SB_W, SB_W, SB_W)
LN_EPS = 1e-5
RMS_EPS = 1e-6
L2_EPS = 1e-6

LANES = 128
SUBLANES = 8
BLK = 128
VMEM_LIMIT = 52 * 1024 * 1024
COL_C = 0
COL_ZG = COL_C + CONV_DIM
COL_Q = COL_ZG + GDN_VW
COL_K = COL_Q + SB_W
COL_V = COL_K + SB_W
COL_ZS = COL_V + SB_W
COL_AB = COL_ZS + SB_W
PROJ_COLS = COL_AB + LANES


def _dot(a, b, precision=None):
    return jnp.dot(a, b, preferred_element_type=F32, precision=precision)


def _dot_nt(a, b, precision=None):
    return lax.dot_general(a, b, (((1,), (1,)), ((), ())),
                           preferred_element_type=F32, precision=precision)


def _layernorm(x, g, b):
    mu = jnp.mean(x, axis=-1, keepdims=True)
    xc = x - mu
    var = jnp.mean(xc * xc, axis=-1, keepdims=True)
    return xc * lax.rsqrt(var + LN_EPS) * g + b


def _sigmoid(x):
    return 1.0 / (1.0 + jnp.exp(-x))


def _softplus(x):
    return jnp.maximum(x, 0.0) + jnp.log(1.0 + jnp.exp(-jnp.abs(x)))


def _project_kernel(n_head, *refs):
    if n_head:
        x_ref, head_ref, g_ref, b_ref, w_ref = refs[:5]
        (c_ref, zg_ref, ab_ref, q_ref, zs_ref, kt32_ref, vt32_ref, kt_ref, vb_ref,
         tail_ref, kv_ref) = refs[5:]

        @pl.when(pl.program_id(0) == 0)
        def _():
            tail_ref[...] = head_ref[...]

        xb = x_ref[...]
        x = jnp.concatenate([tail_ref[...], xb[:BLK - n_head]], axis=0)
        tail_ref[...] = xb[BLK - n_head:]
    else:
        x_ref, g_ref, b_ref, w_ref = refs[:4]
        c_ref, zg_ref, ab_ref, q_ref, zs_ref, k_ref, v_ref = refs[4:]
        x = x_ref[...]
    h = _layernorm(x, g_ref[...], b_ref[...]).astype(BF16)

    def proj(col, width):
        return _dot(h, w_ref[:, col:col + width])

    c_ref[...] = proj(COL_C, CONV_DIM)
    zg_ref[...] = proj(COL_ZG, GDN_VW)
    ab_ref[...] = proj(COL_AB, LANES)
    q_ref[...] = proj(COL_Q, SB_W) * (SB_HEAD_DIM ** -0.5)
    zs_ref[...] = proj(COL_ZS, SB_W)
    k = proj(COL_K, SB_W)
    v = proj(COL_V, SB_W)
    if n_head:
        kv_ref[0] = k
        kv_ref[1] = v
        kt = kv_ref[0].T
        kt32_ref[...] = kt
        vt32_ref[...] = kv_ref[1].T
        kt_ref[0] = kt.astype(BF16)
        vb_ref[0] = v.astype(BF16)
    else:
        k_ref[...] = k
        v_ref[...] = v


def _project(x, head, ln_g, ln_b, w):
    n_head = 0 if head is None else head.shape[0]
    n_x_blk = x.shape[0] // BLK
    n_blk = n_x_blk + (1 if n_head else 0)
    n = n_blk * BLK
    length = x.shape[0] + n_head

    def row(width):
        return pl.BlockSpec((BLK, width), lambda i: (i, 0))

    def const(shape):
        return pl.BlockSpec(shape, lambda i: (0,) * len(shape))

    in_specs = [pl.BlockSpec((BLK, D_MODEL), lambda i: (jnp.minimum(i, n_x_blk - 1), 0))]
    args = [x]
    if n_head:
        in_specs.append(const((n_head, D_MODEL)))
        args.append(head)
    in_specs += [const((1, D_MODEL)), const((1, D_MODEL)), const((D_MODEL, PROJ_COLS))]
    args += [ln_g, ln_b, w]
    out_shape = [jax.ShapeDtypeStruct((n, width), F32) for width in (CONV_DIM, GDN_VW, LANES, SB_W, SB_W)]
    out_specs = [row(width) for width in (CONV_DIM, GDN_VW, LANES, SB_W, SB_W)]
    scratch = []
    if n_head:
        out_shape += [jax.ShapeDtypeStruct((SB_W, length), F32),
                      jax.ShapeDtypeStruct((SB_W, length), F32),
                      jax.ShapeDtypeStruct((n_blk, SB_W, BLK), BF16),
                      jax.ShapeDtypeStruct((n_blk, BLK, SB_W), BF16)]
        out_specs += [pl.BlockSpec((SB_W, BLK), lambda i: (0, i)),
                      pl.BlockSpec((SB_W, BLK), lambda i: (0, i)),
                      pl.BlockSpec((1, SB_W, BLK), lambda i: (i, 0, 0)),
                      pl.BlockSpec((1, BLK, SB_W), lambda i: (i, 0, 0))]
        scratch = [pltpu.VMEM((n_head, D_MODEL), F32), pltpu.VMEM((2, BLK, SB_W), F32)]
    else:
        out_shape += [jax.ShapeDtypeStruct((n, SB_W), F32)] * 2
        out_specs += [row(SB_W), row(SB_W)]
    return pl.pallas_call(
        functools.partial(_project_kernel, n_head),
        grid=(n_blk,),
        in_specs=in_specs,
        out_specs=out_specs,
        out_shape=out_shape,
        scratch_shapes=scratch,
        compiler_params=pltpu.CompilerParams(
            dimension_semantics=("arbitrary",), vmem_limit_bytes=VMEM_LIMIT),
        name="project",
    )(*args)


def _gdn_kernel(chunk, hi, c_ref, ab_ref, cw_ref, alog_ref, dt_ref, s0_ref, conv0_ref,
                o_ref, s_ref, xs_ref):
    j = pl.program_id(1)
    C = chunk

    @pl.when(j == 0)
    def _():
        xs_ref[0:SUBLANES, :] = conv0_ref[0]
        s_ref[...] = s0_ref[...]

    valid = j * C + lax.broadcasted_iota(jnp.int32, (C, 1), 0) < hi

    xs_ref[SUBLANES:SUBLANES + C, :] = jnp.where(valid, c_ref[...], 0.0)
    first = SUBLANES - (CONV_W - 1)
    y = xs_ref[first:first + C, :] * cw_ref[0:1, :]
    for i in range(1, CONV_W):
        y = y + xs_ref[first + i:first + i + C, :] * cw_ref[i:i + 1, :]
    xs_ref[0:SUBLANES, :] = xs_ref[C:C + SUBLANES, :]
    y = y * _sigmoid(y)

    ab = ab_ref[...]
    g = jnp.where(valid, -jnp.exp(alog_ref[...]) * _softplus(ab + dt_ref[...]), 0.0)
    beta = jnp.where(valid, _sigmoid(ab), 0.0)
    ri = lax.broadcasted_iota(jnp.int32, (C, C), 0)
    ci = lax.broadcasted_iota(jnp.int32, (C, C), 1)
    causal = ri >= ci
    strict = ri > ci
    gc = _dot(causal.astype(F32), g, HIGHEST)
    gct = gc.T
    eye = (ri == ci).astype(F32)

    for h in range(GDN_HEADS):
        q = y[:, h * GDN_DK:(h + 1) * GDN_DK]
        k = y[:, GDN_QK + h * GDN_DK:GDN_QK + (h + 1) * GDN_DK]
        v = y[:, 2 * GDN_QK + h * GDN_DV:2 * GDN_QK + (h + 1) * GDN_DV]
        q = q * lax.rsqrt(jnp.sum(q * q, axis=-1, keepdims=True) + L2_EPS) * (GDN_DK ** -0.5)
        k = k * lax.rsqrt(jnp.sum(k * k, axis=-1, keepdims=True) + L2_EPS)
        gcol = gc[:, h:h + 1]
        grow = gct[h:h + 1, :]
        bcol = beta[:, GDN_HEADS + h:GDN_HEADS + h + 1]
        glast = gc[C - 1:C, h:h + 1]
        decay = jnp.exp(jnp.where(causal, gcol - grow, -jnp.inf))
        kb = k * bcol
        m = jnp.where(strict, _dot_nt(kb, k, HIGHEST) * decay, 0.0)
        p = -m
        t = eye + p
        for _ in range(int(math.log2(C)) - 1):
            p = _dot(p, p, HIGHEST)
            t = t + _dot(t, p, HIGHEST)
        eg = jnp.exp(gcol)
        rhs = jnp.concatenate([v * bcol, kb * eg], axis=-1)
        sol = _dot(t, rhs, HIGHEST)
        u = sol[:, :GDN_DV]
        w = sol[:, GDN_DV:]
        qk = _dot_nt(q, k, HIGHEST) * decay
        qd = q * eg
        kdt = (k * jnp.exp(glast - gcol)).T
        s = s_ref[0, h]
        v_new = u - _dot(w, s, HIGHEST)
        o_ref[:, h * GDN_DV:(h + 1) * GDN_DV] = _dot(qd, s, HIGHEST) + _dot(qk, v_new, HIGHEST)
        s_ref[0, h] = s * jnp.exp(glast) + _dot(kdt, v_new, HIGHEST)


def _gdn(c, ab, conv_w, a_log, dt_bias, s0, conv0, rows_per_seq, hi):
    b = s0.shape[0]
    chunk = GDN_CHUNK
    n_chunks = rows_per_seq // chunk
    return pl.pallas_call(
        functools.partial(_gdn_kernel, chunk, hi),
        grid=(b, n_chunks),
        in_specs=[
            pl.BlockSpec((chunk, CONV_DIM), lambda bi, j: (bi * n_chunks + j, 0)),
            pl.BlockSpec((chunk, LANES), lambda bi, j: (bi * n_chunks + j, 0)),
            pl.BlockSpec((SUBLANES, CONV_DIM), lambda bi, j: (0, 0)),
            pl.BlockSpec((1, LANES), lambda bi, j: (0, 0)),
            pl.BlockSpec((1, LANES), lambda bi, j: (0, 0)),
            pl.BlockSpec((1, GDN_HEADS, GDN_DK, GDN_DV), lambda bi, j: (bi, 0, 0, 0)),
            pl.BlockSpec((1, SUBLANES, CONV_DIM), lambda bi, j: (bi, 0, 0)),
        ],
        out_specs=[
            pl.BlockSpec((chunk, GDN_VW), lambda bi, j: (bi * n_chunks + j, 0)),
            pl.BlockSpec((1, GDN_HEADS, GDN_DK, GDN_DV), lambda bi, j: (bi, 0, 0, 0)),
        ],
        out_shape=[
            jax.ShapeDtypeStruct((b * rows_per_seq, GDN_VW), F32),
            jax.ShapeDtypeStruct((b, GDN_HEADS, GDN_DK, GDN_DV), F32),
        ],
        scratch_shapes=[pltpu.VMEM((chunk + SUBLANES, CONV_DIM), F32)],
        compiler_params=pltpu.CompilerParams(
            dimension_semantics=("arbitrary", "arbitrary"), vmem_limit_bytes=VMEM_LIMIT),
        name="gdn",
    )(c, ab, conv_w, a_log, dt_bias, s0, conv0)


def _cumsum_weights():
    r = jnp.arange(2 * BLK)[:, None] % BLK
    c = jnp.arange(2 * BLK)[None, :]
    return jnp.where(c < BLK, r > c, True).astype(BF16)


def _sb_tile(t, vis, carry, tw):
    sp = _softplus(t)
    if vis is not None:
        sp = jnp.where(vis, sp, 0.0)
    hi = sp.astype(BF16)
    lo = (sp - hi.astype(F32)).astype(BF16)
    res = _dot(jnp.concatenate([hi, lo], axis=-1), tw)
    a = jnp.exp(t - sp - res[:, :BLK] - carry)
    if vis is not None:
        a = jnp.where(vis, a, 0.0)
    return a, carry + res[:, BLK:]


def _sb_prompt_kernel(q_ref, kt_ref, vb_ref, bias_ref, tw_ref, o_ref, carry_ref, acc_ref):
    i = pl.program_id(0)
    tw = tw_ref[...]
    qidx = lax.broadcasted_iota(jnp.int32, (BLK, 2 * BLK), 0)
    kidx = lax.broadcasted_iota(jnp.int32, (BLK, 2 * BLK), 1) % BLK
    diag_vis = kidx < qidx
    lane = lax.broadcasted_iota(jnp.int32, (BLK, 2 * SB_HEAD_DIM), 1)
    zeros_k = jnp.zeros((SB_HEAD_DIM, BLK), BF16)

    for p in range(SB_HEADS // 2):
        cols = slice(2 * SB_HEAD_DIM * p, 2 * SB_HEAD_DIM * (p + 1))
        qp = q_ref[:, cols].astype(BF16)
        bias2 = bias_ref[:, 2 * BLK * p:2 * BLK * (p + 1)]
        carry_ref[...] = jnp.zeros_like(carry_ref)
        acc_ref[...] = jnp.zeros_like(acc_ref)

        def step(j, masked):
            ktp = kt_ref[j, cols, :]
            w = jnp.concatenate([
                jnp.concatenate([ktp[:SB_HEAD_DIM], zeros_k], axis=1),
                jnp.concatenate([zeros_k, ktp[SB_HEAD_DIM:]], axis=1)], axis=0)
            t = _dot(qp, w) + bias2
            a = []
            for hh in range(2):
                sl = slice(hh * BLK, (hh + 1) * BLK)
                a_h, c_h = _sb_tile(t[:, sl], diag_vis[:, sl] if masked else None, carry_ref[hh], tw)
                carry_ref[hh] = c_h
                a.append(a_h.astype(BF16))
            vp = vb_ref[j, :, cols]
            vbd = jnp.concatenate([jnp.where(lane < SB_HEAD_DIM, vp, 0),
                                   jnp.where(lane >= SB_HEAD_DIM, vp, 0)], axis=0)
            acc_ref[...] += _dot(jnp.concatenate(a, axis=1), vbd)

        step(i, True)

        def body(jj, _):
            step(i - 1 - jj, False)
            return 0

        lax.fori_loop(0, i, body, 0)
        o_ref[:, cols] = acc_ref[...]


def _sb_prompt(q, kt, vb, bias2, tw):
    n_blk = kt.shape[0]
    return pl.pallas_call(
        _sb_prompt_kernel,
        grid=(n_blk,),
        in_specs=[
            pl.BlockSpec((BLK, SB_W), lambda i: (i, 0)),
            pl.BlockSpec((n_blk, SB_W, BLK), lambda i: (0, 0, 0), pipeline_mode=pl.Buffered(1)),
            pl.BlockSpec((n_blk, BLK, SB_W), lambda i: (0, 0, 0), pipeline_mode=pl.Buffered(1)),
            pl.BlockSpec((1, SB_HEADS * BLK), lambda i: (0, 0)),
            pl.BlockSpec((2 * BLK, 2 * BLK), lambda i: (0, 0)),
        ],
        out_specs=pl.BlockSpec((BLK, SB_W), lambda i: (i, 0)),
        out_shape=jax.ShapeDtypeStruct((n_blk * BLK, SB_W), F32),
        scratch_shapes=[pltpu.VMEM((2, BLK, BLK), F32), pltpu.VMEM((BLK, 2 * SB_HEAD_DIM), F32)],
        compiler_params=pltpu.CompilerParams(
            dimension_semantics=("arbitrary",), vmem_limit_bytes=VMEM_LIMIT),
        name="sb_prompt",
    )(q, kt, vb, bias2, tw)


def _sb_decode_kernel(t_new, pages_per_step, pt_ref, q_ref, kn_ref, vn_ref, bias_ref, tw_ref, *refs):
    k_refs = refs[:pages_per_step]
    v_refs = refs[pages_per_step:2 * pages_per_step]
    o_ref, qbd_ref, carry_ref, acc_ref = refs[2 * pages_per_step:]
    j = pl.program_id(1)
    rows = SB_HEADS * t_new
    tw = tw_ref[...]
    bias = bias_ref[...]
    head_of_lane = lax.broadcasted_iota(jnp.int32, (t_new, SB_W), 1) // SB_HEAD_DIM

    def block(kt_bf, vt_bf, vis):
        t = _dot(qbd_ref[...], kt_bf) + bias
        a, c = _sb_tile(t, vis, carry_ref[...], tw)
        carry_ref[...] = c
        acc_ref[...] += _dot_nt(a.astype(BF16), vt_bf)

    @pl.when(j == 0)
    def _():
        q = q_ref[0]
        qbd_ref[...] = jnp.concatenate(
            [jnp.where(head_of_lane == h, q, 0.0) for h in range(SB_HEADS)], axis=0).astype(BF16)
        carry_ref[...] = jnp.zeros_like(carry_ref)
        acc_ref[...] = jnp.zeros_like(acc_ref)
        pad_rows = jnp.zeros((BLK - t_new, SB_W), F32)
        knt = jnp.concatenate([kn_ref[0], pad_rows], axis=0).T.astype(BF16)
        vnt = jnp.concatenate([vn_ref[0], pad_rows], axis=0).T.astype(BF16)
        tq = lax.broadcasted_iota(jnp.int32, (rows, BLK), 0) % t_new
        ks = lax.broadcasted_iota(jnp.int32, (rows, BLK), 1)
        block(knt, vnt, ks < tq)

    for r in range(pages_per_step):
        block(k_refs[r][0].astype(BF16), v_refs[r][0].astype(BF16), None)

    @pl.when(j == pl.num_programs(1) - 1)
    def _():
        acc = acc_ref[...]
        out = jnp.zeros((t_new, SB_W), F32)
        for h in range(SB_HEADS):
            out = out + jnp.where(head_of_lane == h, acc[h * t_new:(h + 1) * t_new], 0.0)
        o_ref[0] = out


def _sb_decode(q, k_new, v_new, cache_kt, cache_vt, page_table, bias_rows, tw, pages_per_step):
    b, t_new, _ = q.shape
    n_pages = page_table.shape[1]
    page = cache_kt.shape[2]
    assert page == BLK and n_pages % pages_per_step == 0
    n_steps = n_pages // pages_per_step
    rows = SB_HEADS * t_new

    def page_spec(r):
        return pl.BlockSpec(
            (1, SB_W, page),
            lambda bi, j, pt: (pt[bi, n_pages - 1 - (j * pages_per_step + r)], 0, 0))

    seq = pl.BlockSpec((1, t_new, SB_W), lambda bi, j, pt: (bi, 0, 0))
    grid_spec = pltpu.PrefetchScalarGridSpec(
        num_scalar_prefetch=1,
        grid=(b, n_steps),
        in_specs=[seq, seq, seq,
                  pl.BlockSpec((rows, BLK), lambda bi, j, pt: (0, 0)),
                  pl.BlockSpec((2 * BLK, 2 * BLK), lambda bi, j, pt: (0, 0))]
        + [page_spec(r) for r in range(pages_per_step)]
        + [page_spec(r) for r in range(pages_per_step)],
        out_specs=seq,
        scratch_shapes=[pltpu.VMEM((rows, SB_W), BF16),
                        pltpu.VMEM((rows, BLK), F32),
                        pltpu.VMEM((rows, SB_W), F32)],
    )
    return pl.pallas_call(
        functools.partial(_sb_decode_kernel, t_new, pages_per_step),
        grid_spec=grid_spec,
        out_shape=jax.ShapeDtypeStruct((b, t_new, SB_W), F32),
        compiler_params=pltpu.CompilerParams(
            dimension_semantics=("arbitrary", "arbitrary"), vmem_limit_bytes=VMEM_LIMIT),
        name="sb_decode",
    )(page_table, q, k_new, v_new, bias_rows, tw,
      *([cache_kt] * pages_per_step), *([cache_vt] * pages_per_step))


def _merge_kernel(alpha, shift, *refs):
    x_ref = refs[0]
    n_mix = 4
    if shift:
        mixer = [jnp.concatenate([refs[1 + 2 * m][shift:, :], refs[2 + 2 * m][...]], axis=0)
                 for m in range(n_mix)]
        refs = refs[1 + 2 * n_mix:]
    else:
        mixer = [refs[1 + m][...] for m in range(n_mix)]
        refs = refs[1 + n_mix:]
    og, zg, osb, zs = mixer
    lig_ref, lib_ref, ng_ref, w_ref, log_ref, lob_ref, y_ref = refs
    h = _layernorm(x_ref[...], lig_ref[...], lib_ref[...])
    parts = []
    for hd in range(GDN_HEADS):
        sl = slice(hd * GDN_DV, (hd + 1) * GDN_DV)
        o = og[:, sl]
        o = o * lax.rsqrt(jnp.mean(o * o, axis=-1, keepdims=True) + RMS_EPS) * ng_ref[...]
        z = zg[:, sl]
        parts.append(o * (z * _sigmoid(z)))
    parts.append(osb * (zs * _sigmoid(zs)))
    cat = jnp.concatenate(parts, axis=-1).astype(BF16)
    mix = _dot(cat, w_ref[...])
    y_ref[...] = _layernorm(alpha * h + mix, log_ref[...], lob_ref[...])


def _merge(x, mixer, ln_in_g, ln_in_b, norm_g, w_out, ln_out_g, ln_out_b, shift, alpha):
    n = x.shape[0]
    n_blk = n // BLK
    assert BLK % shift == 0 if shift else True
    vec = pl.BlockSpec((1, D_MODEL), lambda i: (0, 0))
    in_specs = [pl.BlockSpec((BLK, D_MODEL), lambda i: (i, 0))]
    args = [x]
    for m in mixer:
        in_specs.append(pl.BlockSpec((BLK, m.shape[1]), lambda i: (i, 0)))
        args.append(m)
        if shift:
            per_blk = BLK // shift
            in_specs.append(pl.BlockSpec((shift, m.shape[1]), lambda i: ((i + 1) * per_blk, 0)))
            args.append(m)
    in_specs += [vec, vec, pl.BlockSpec((1, GDN_DV), lambda i: (0, 0)),
                 pl.BlockSpec((D_MODEL, D_MODEL), lambda i: (0, 0)), vec, vec]
    args += [ln_in_g, ln_in_b, norm_g, w_out, ln_out_g, ln_out_b]
    return pl.pallas_call(
        functools.partial(_merge_kernel, alpha, shift),
        grid=(n_blk,),
        in_specs=in_specs,
        out_specs=pl.BlockSpec((BLK, D_MODEL), lambda i: (i, 0)),
        out_shape=jax.ShapeDtypeStruct((n, D_MODEL), F32),
        compiler_params=pltpu.CompilerParams(
            dimension_semantics=("arbitrary",), vmem_limit_bytes=VMEM_LIMIT),
        name="merge",
    )(*args)


def _reorder_w_in(w_in):
    offs = [0]
    for s in PROJ_SIZES:
        offs.append(offs[-1] + s)
    c, zg, a, b, q, k, v, zs = [w_in[:, offs[i]:offs[i + 1]] for i in range(len(PROJ_SIZES))]
    pad = jnp.zeros((w_in.shape[0], LANES - 2 * GDN_HEADS), w_in.dtype)
    return jnp.concatenate([c, zg, q, k, v, zs, a, b, pad], axis=1).astype(BF16)


def _lane_row(v, width=LANES):
    return jnp.pad(v.astype(F32), (0, width - v.shape[0])).reshape(1, width)


def kernel(x_prompt, x_sample, cache_k, cache_v, page_table, state_gdn, state_conv, meta_tokens,
           ln_in_g, ln_in_b, w_in, conv_w, a_log, dt_bias, gdn_norm_g, sb_bias, w_out, ln_out_g,
           ln_out_b):
    depth = w_in.shape[0]
    assert depth == 1, "single-layer trunk only"
    batch, seq, _ = x_prompt.shape
    dec_batch, dec_seq, _ = x_sample.shape
    assert batch == 1 and seq % BLK == 0 and meta_tokens.shape[0] == N_META
    assert (dec_batch * dec_seq) % BLK == 0 and dec_seq <= GDN_CHUNK
    alpha = (2 * depth) ** 0.25
    length = N_META + seq
    n_pad = BLK + seq

    lig = ln_in_g.reshape(1, D_MODEL)
    lib = ln_in_b.reshape(1, D_MODEL)
    w = _reorder_w_in(w_in[0])
    w_o = w_out[0].astype(BF16)
    cw = jnp.pad(conv_w[0], ((0, SUBLANES - CONV_W), (0, 0)))
    alog = _lane_row(a_log[0])
    dtb = _lane_row(dt_bias[0])
    norm_g = gdn_norm_g[0].reshape(1, GDN_DV)
    log_ = ln_out_g[0].reshape(1, D_MODEL)
    lob = ln_out_b[0].reshape(1, D_MODEL)
    tw = _cumsum_weights()
    bias = sb_bias[0].astype(F32)

    xp = x_prompt[0]
    c_p, zg_p, ab_p, q_p, zs_p, kt32_p, vt32_p, kt_p, vb_p = _project(
        xp, meta_tokens.astype(F32), lig, lib, w)
    og_p, s_p = _gdn(c_p, ab_p, cw, alog, dtb,
                     jnp.zeros((1, GDN_HEADS, GDN_DK, GDN_DV), F32),
                     jnp.zeros((1, SUBLANES, CONV_DIM), F32), n_pad, length)
    os_p = _sb_prompt(q_p, kt_p, vb_p, jnp.repeat(bias, BLK).reshape(1, SB_HEADS * BLK), tw)
    y_p = _merge(xp, (og_p, zg_p, os_p, zs_p), lig, lib, norm_g, w_o, log_, lob, N_META, alpha)

    xs = x_sample.reshape(dec_batch * dec_seq, D_MODEL)
    c_s, zg_s, ab_s, q_s, zs_s, k_s, v_s = _project(xs, None, lig, lib, w)

    def per_seq_chunk(a):
        a = a.reshape(dec_batch, dec_seq, a.shape[-1])
        a = jnp.pad(a, ((0, 0), (0, GDN_CHUNK - dec_seq), (0, 0)))
        return a.reshape(dec_batch * GDN_CHUNK, a.shape[-1])

    conv0 = jnp.pad(state_conv[0].astype(F32), ((0, 0), (SUBLANES - (CONV_W - 1), 0), (0, 0)))
    og_s, s_s = _gdn(per_seq_chunk(c_s), per_seq_chunk(ab_s), cw, alog, dtb,
                     state_gdn[0].astype(F32), conv0, GDN_CHUNK, dec_seq)
    og_s = og_s.reshape(dec_batch, GDN_CHUNK, GDN_VW)[:, :dec_seq].reshape(dec_batch * dec_seq, GDN_VW)
    n_phys, page = cache_k.shape[1], cache_k.shape[2]

    def pages_t(cache):
        return jnp.transpose(cache, (0, 2, 3, 1)).reshape(n_phys, SB_W, page)

    os_s = _sb_decode(q_s.reshape(dec_batch, dec_seq, SB_W), k_s.reshape(dec_batch, dec_seq, SB_W),
                      v_s.reshape(dec_batch, dec_seq, SB_W),
                      pages_t(cache_k[0]), pages_t(cache_v[0]), page_table,
                      jnp.broadcast_to(jnp.repeat(bias, dec_seq)[:, None], (SB_HEADS * dec_seq, BLK)),
                      tw, 4)
    y_s = _merge(xs, (og_s, zg_s, os_s.reshape(dec_batch * dec_seq, SB_W), zs_s), lig, lib, norm_g,
                 w_o, log_, lob, 0, alpha)

    def heads_last(t):
        return jnp.transpose(t.reshape(SB_HEADS, SB_HEAD_DIM, length), (2, 0, 1))[None, None]

    kv_s = (1, dec_batch, dec_seq, SB_HEADS, SB_HEAD_DIM)
    c_s3 = c_s.reshape(dec_batch, dec_seq, CONV_DIM)
    conv_s = jnp.concatenate([state_conv[0].astype(F32), c_s3], axis=1)[:, -(CONV_W - 1):]
    return (y_p[None], y_s.reshape(dec_batch, dec_seq, D_MODEL),
            heads_last(kt32_p), heads_last(vt32_p),
            k_s.reshape(kv_s), v_s.reshape(kv_s),
            s_p[None], s_s[None],
            c_p[length - (CONV_W - 1):length].reshape(1, 1, CONV_W - 1, CONV_DIM), conv_s[None])
```

```python
import functools
import math

import jax
import jax.numpy as jnp
from jax import lax
from jax.experimental import pallas as pl
from jax.experimental.pallas import tpu as pltpu

F32 = jnp.float32
BF16 = jnp.bfloat16

D_MODEL = 1024
N_META = 16
GDN_HEADS = 4
GDN_DK = 128
GDN_DV = 128
GDN_CHUNK = 64
CONV_W = 4
SB_HEADS = 8
SB_HEAD_DIM = 64
GDN_QK = GDN_HEADS * GDN_DK
GDN_VW = GDN_HEADS * GDN_DV
CONV_DIM = 2 * GDN_QK + GDN_VW
SB_W = SB_HEADS * SB_HEAD_DIM
PROJ_SIZES = (CONV_DIM, GDN_VW, GDN_HEADS, GDN_HEADS, SB_W, SB_W, SB_W, SB_W)
LN_EPS = 1e-5
RMS_EPS = 1e-6
L2_EPS = 1e-6
LOG2E = 1.4426950408889634

LANES = 128
SUBLANES = 8
BLK = 128
SB_BQ = 256
DECODE_PAGES_PER_STEP = 8
GDN_CHUNKS_PER_STEP = 4
VMEM_LIMIT = 52 * 1024 * 1024
COL_C = 0
COL_ZG = COL_C + CONV_DIM
COL_Q = COL_ZG + GDN_VW
COL_K = COL_Q + SB_W
COL_V = COL_K + SB_W
COL_ZS = COL_V + SB_W
COL_AB = COL_ZS + SB_W
PROJ_COLS = COL_AB + LANES


def _dot(a, b):
    return jnp.dot(a, b, preferred_element_type=F32)


def _dot_nt(a, b):
    return lax.dot_general(a, b, (((1,), (1,)), ((), ())), preferred_element_type=F32)


def _split2(x):
    hi = x.astype(BF16)
    return hi, (x - hi.astype(F32)).astype(BF16)


def _dot1(a, b):
    return _dot(a.astype(BF16), b.astype(BF16))


def _dot1_nt(a, b):
    return _dot_nt(a.astype(BF16), b.astype(BF16))


def _dot3(a, b):
    a_hi, a_lo = _split2(a)
    b_hi, b_lo = _split2(b)
    return _dot(a_hi, b_hi) + (_dot(a_hi, b_lo) + _dot(a_lo, b_hi))


def _layernorm(x, g, b):
    mu = jnp.mean(x, axis=-1, keepdims=True)
    xc = x - mu
    var = jnp.mean(xc * xc, axis=-1, keepdims=True)
    return xc * lax.rsqrt(var + LN_EPS) * g + b


def _sigmoid(x):
    return 1.0 / (1.0 + jnp.exp(-x))


def _softplus(x):
    return jnp.maximum(x, 0.0) + jnp.log(1.0 + jnp.exp(-jnp.abs(x)))


def _project_kernel(n_head, q_scale, *refs):
    if n_head:
        x_ref, head_ref, g_ref, b_ref, w_ref = refs[:5]
        (c_ref, zg_ref, ab_ref, q_ref, zs_ref, kt32_ref, vt32_ref, kt_ref, vb_ref,
         tail_ref, kv_ref) = refs[5:]

        @pl.when(pl.program_id(0) == 0)
        def _():
            tail_ref[...] = head_ref[...]

        xb = x_ref[...]
        x = jnp.concatenate([tail_ref[...], xb[:BLK - n_head]], axis=0)
        tail_ref[...] = xb[BLK - n_head:]
    else:
        x_ref, g_ref, b_ref, w_ref = refs[:4]
        c_ref, zg_ref, ab_ref, q_ref, zs_ref, k_ref, v_ref = refs[4:]
        x = x_ref[...]
    h = _layernorm(x, g_ref[...], b_ref[...]).astype(BF16)

    def proj(col, width):
        return _dot(h, w_ref[:, col:col + width])

    c_ref[...] = proj(COL_C, CONV_DIM)
    zg_ref[...] = proj(COL_ZG, GDN_VW)
    ab_ref[...] = proj(COL_AB, LANES)
    q_ref[...] = proj(COL_Q, SB_W) * q_scale
    zs_ref[...] = proj(COL_ZS, SB_W)
    k = proj(COL_K, SB_W)
    v = proj(COL_V, SB_W)
    if n_head:
        kv_ref[0] = k
        kv_ref[1] = v
        kt = kv_ref[0].T
        kt32_ref[...] = kt
        vt32_ref[...] = kv_ref[1].T
        kt_ref[0] = kt.astype(BF16)
        vb_ref[0] = v.astype(BF16)
    else:
        k_ref[...] = k
        v_ref[...] = v


def _project(x, head, ln_g, ln_b, w, q_scale, n_blk):
    n_head = 0 if head is None else head.shape[0]
    n_x_blk = x.shape[0] // BLK
    n = n_blk * BLK
    length = x.shape[0] + n_head
    last_t = (length - 1) // BLK

    def row(width):
        return pl.BlockSpec((BLK, width), lambda i: (i, 0))

    def const(shape):
        return pl.BlockSpec(shape, lambda i: (0,) * len(shape))

    in_specs = [pl.BlockSpec((BLK, D_MODEL), lambda i: (jnp.minimum(i, n_x_blk - 1), 0))]
    args = [x]
    if n_head:
        in_specs.append(const((n_head, D_MODEL)))
        args.append(head)
    in_specs += [const((1, D_MODEL)), const((1, D_MODEL)), const((D_MODEL, PROJ_COLS))]
    args += [ln_g, ln_b, w]
    out_shape = [jax.ShapeDtypeStruct((n, width), F32) for width in (CONV_DIM, GDN_VW, LANES, SB_W, SB_W)]
    out_specs = [row(width) for width in (CONV_DIM, GDN_VW, LANES, SB_W, SB_W)]
    scratch = []
    if n_head:
        out_shape += [jax.ShapeDtypeStruct((SB_W, length), F32),
                      jax.ShapeDtypeStruct((SB_W, length), F32),
                      jax.ShapeDtypeStruct((n_blk, SB_W, BLK), BF16),
                      jax.ShapeDtypeStruct((n_blk, BLK, SB_W), BF16)]
        out_specs += [pl.BlockSpec((SB_W, BLK), lambda i: (0, jnp.minimum(i, last_t))),
                      pl.BlockSpec((SB_W, BLK), lambda i: (0, jnp.minimum(i, last_t))),
                      pl.BlockSpec((1, SB_W, BLK), lambda i: (i, 0, 0)),
                      pl.BlockSpec((1, BLK, SB_W), lambda i: (i, 0, 0))]
        scratch = [pltpu.VMEM((n_head, D_MODEL), F32), pltpu.VMEM((2, BLK, SB_W), F32)]
    else:
        out_shape += [jax.ShapeDtypeStruct((n, SB_W), F32)] * 2
        out_specs += [row(SB_W), row(SB_W)]
    return pl.pallas_call(
        functools.partial(_project_kernel, n_head, q_scale),
        grid=(n_blk,),
        in_specs=in_specs,
        out_specs=out_specs,
        out_shape=out_shape,
        scratch_shapes=scratch,
        compiler_params=pltpu.CompilerParams(
            dimension_semantics=("arbitrary",), vmem_limit_bytes=VMEM_LIMIT),
        name="project",
    )(*args)


def _gdn_kernel(chunk, n_sub, hi, c_ref, ab_ref, cw_ref, alog_ref, dt_ref, s0_ref, conv0_ref,
                o_ref, s_ref, xs_ref):
    j = pl.program_id(1)
    C = chunk
    R = C * n_sub

    @pl.when(j == 0)
    def _():
        xs_ref[0:SUBLANES, :] = conv0_ref[0]
        s_ref[...] = s0_ref[...]

    valid = j * R + lax.broadcasted_iota(jnp.int32, (R, 1), 0) < hi

    xs_ref[SUBLANES:SUBLANES + R, :] = jnp.where(valid, c_ref[...], 0.0)
    first = SUBLANES - (CONV_W - 1)
    y = xs_ref[first:first + R, :] * cw_ref[0:1, :]
    for i in range(1, CONV_W):
        y = y + xs_ref[first + i:first + i + R, :] * cw_ref[i:i + 1, :]
    xs_ref[0:SUBLANES, :] = xs_ref[R:R + SUBLANES, :]
    y = y * _sigmoid(y)

    ab = ab_ref[...]
    g_all = jnp.where(valid, -jnp.exp(alog_ref[...]) * _softplus(ab + dt_ref[...]), 0.0)
    beta_all = jnp.where(valid, _sigmoid(ab), 0.0)
    ri = lax.broadcasted_iota(jnp.int32, (C, C), 0)
    ci = lax.broadcasted_iota(jnp.int32, (C, C), 1)
    causal = ri >= ci
    strict = ri > ci
    tri = causal.astype(BF16)
    eye = (ri == ci).astype(F32)

    prep = []
    for cc in range(n_sub):
        rows = slice(cc * C, (cc + 1) * C)
        g = g_all[rows]
        beta = beta_all[rows]
        g_hi = g.astype(BF16)
        g_rest = g - g_hi.astype(F32)
        g_mid, g_lo = _split2(g_rest)
        gc = _dot(tri, g_hi) + (_dot(tri, g_mid) + _dot(tri, g_lo))
        gct = gc.T
        for h in range(GDN_HEADS):
            q = y[rows, h * GDN_DK:(h + 1) * GDN_DK]
            k = y[rows, GDN_QK + h * GDN_DK:GDN_QK + (h + 1) * GDN_DK]
            v = y[rows, 2 * GDN_QK + h * GDN_DV:2 * GDN_QK + (h + 1) * GDN_DV]
            q = q * lax.rsqrt(jnp.sum(q * q, axis=-1, keepdims=True) + L2_EPS) * (GDN_DK ** -0.5)
            k = k * lax.rsqrt(jnp.sum(k * k, axis=-1, keepdims=True) + L2_EPS)
            gcol = gc[:, h:h + 1]
            grow = gct[h:h + 1, :]
            bcol = beta[:, GDN_HEADS + h:GDN_HEADS + h + 1]
            glast = gc[C - 1:C, h:h + 1]
            decay = jnp.exp(jnp.where(causal, gcol - grow, -jnp.inf))
            kb = k * bcol
            eg = jnp.exp(gcol)
            prep.append(dict(
                m=jnp.where(strict, _dot1_nt(kb, k) * decay, 0.0),
                rhs=jnp.concatenate([v * bcol, kb * eg], axis=-1),
                qk=_dot1_nt(q, k) * decay, qd=q * eg,
                kdt=(k * jnp.exp(glast - gcol)).T,
                eg_last=jnp.exp(glast)))

    ps = [-pr["m"] for pr in prep]
    ts = [eye + p for p in ps]
    for _ in range(int(math.log2(C)) - 1):
        ps = [_dot3(p, p) for p in ps]
        ts = [t + _dot3(t, p) for t, p in zip(ts, ps)]
    sols = [_dot3(t, pr["rhs"]) for t, pr in zip(ts, prep)]

    for cc in range(n_sub):
        ids = [cc * GDN_HEADS + h for h in range(GDN_HEADS)]
        s = [s_ref[0, h] for h in range(GDN_HEADS)]
        v_new = [sols[i][:, :GDN_DV] - _dot1(sols[i][:, GDN_DV:], s[h]) for h, i in enumerate(ids)]
        for h, i in enumerate(ids):
            s_ref[0, h] = s[h] * prep[i]["eg_last"] + _dot1(prep[i]["kdt"], v_new[h])
        for h, i in enumerate(ids):
            o_ref[cc * C:(cc + 1) * C, h * GDN_DV:(h + 1) * GDN_DV] = (
                _dot1(prep[i]["qd"], s[h]) + _dot1(prep[i]["qk"], v_new[h]))


def _gdn(c, ab, conv_w, a_log, dt_bias, s0, conv0, rows_per_seq, hi, n_sub):
    b = s0.shape[0]
    rows = GDN_CHUNK * n_sub
    n_steps = rows_per_seq // rows
    return pl.pallas_call(
        functools.partial(_gdn_kernel, GDN_CHUNK, n_sub, hi),
        grid=(b, n_steps),
        in_specs=[
            pl.BlockSpec((rows, CONV_DIM), lambda bi, j: (bi * n_steps + j, 0)),
            pl.BlockSpec((rows, LANES), lambda bi, j: (bi * n_steps + j, 0)),
            pl.BlockSpec((SUBLANES, CONV_DIM), lambda bi, j: (0, 0)),
            pl.BlockSpec((1, LANES), lambda bi, j: (0, 0)),
            pl.BlockSpec((1, LANES), lambda bi, j: (0, 0)),
            pl.BlockSpec((1, GDN_HEADS, GDN_DK, GDN_DV), lambda bi, j: (bi, 0, 0, 0)),
            pl.BlockSpec((1, SUBLANES, CONV_DIM), lambda bi, j: (bi, 0, 0)),
        ],
        out_specs=[
            pl.BlockSpec((rows, GDN_VW), lambda bi, j: (bi * n_steps + j, 0)),
            pl.BlockSpec((1, GDN_HEADS, GDN_DK, GDN_DV), lambda bi, j: (bi, 0, 0, 0)),
        ],
        out_shape=[
            jax.ShapeDtypeStruct((b * rows_per_seq, GDN_VW), F32),
            jax.ShapeDtypeStruct((b, GDN_HEADS, GDN_DK, GDN_DV), F32),
        ],
        scratch_shapes=[pltpu.VMEM((rows + SUBLANES, CONV_DIM), F32)],
        compiler_params=pltpu.CompilerParams(
            dimension_semantics=("arbitrary", "arbitrary"), vmem_limit_bytes=VMEM_LIMIT),
        name="gdn",
    )(c, ab, conv_w, a_log, dt_bias, s0, conv0)


def _cumsum_weights():
    r = jnp.arange(2 * BLK)[:, None] % BLK
    c = jnp.arange(2 * BLK)[None, :]
    return jnp.where(c < BLK, r > c, True).astype(BF16)


def _softplus2(t2):
    neg_abs = lax.bitcast_convert_type(
        lax.bitcast_convert_type(t2, jnp.uint32) | jnp.uint32(0x80000000), F32)
    return jnp.maximum(t2, 0.0) + jnp.log(1.0 + jnp.exp2(neg_abs)) * LOG2E


def _block_sums(sp, tw):
    hi, lo = _split2(sp)
    res = _dot(jnp.concatenate([hi, lo], axis=-1), tw)
    return res[:, :BLK], res[:, BLK:]


def _sb_tile(t2, vis, carry, tw):
    sp = _softplus2(t2)
    if vis is not None:
        sp = jnp.where(vis, sp, 0.0)
    later, total = _block_sums(sp, tw)
    a = jnp.exp2(t2 - sp - later - carry)
    if vis is not None:
        a = jnp.where(vis, a, 0.0)
    return a, carry + total


def _sb_prompt_kernel(q_ref, kt_ref, vb_ref, bias_ref, tw_ref, o_ref, qb_ref, carry_ref, acc_ref):
    i = pl.program_id(0)
    bq = q_ref.shape[0]
    sub = bq // BLK
    tw = tw_ref[...]
    qidx = i * bq + lax.broadcasted_iota(jnp.int32, (bq, BLK), 0)
    kidx = lax.broadcasted_iota(jnp.int32, (bq, BLK), 1)
    lane = lax.broadcasted_iota(jnp.int32, (BLK, 2 * SB_HEAD_DIM), 1)
    zeros_k = jnp.zeros((SB_HEAD_DIM, BLK), BF16)
    qb_ref[...] = q_ref[...].astype(BF16)
    carry_ref[...] = jnp.zeros_like(carry_ref)
    acc_ref[...] = jnp.zeros_like(acc_ref)

    def step(j, masked):
        vis = (j * BLK + kidx < qidx) if masked else None
        pairs = range(SB_HEADS // 2)
        heads = range(SB_HEADS)

        def pair_cols(p):
            return slice(2 * SB_HEAD_DIM * p, 2 * SB_HEAD_DIM * (p + 1))

        t2p = []
        for p in pairs:
            ktp = kt_ref[j, pair_cols(p), :]
            w = jnp.concatenate([
                jnp.concatenate([ktp[:SB_HEAD_DIM], zeros_k], axis=1),
                jnp.concatenate([zeros_k, ktp[SB_HEAD_DIM:]], axis=1)], axis=0)
            t2p.append(_dot(qb_ref[:, pair_cols(p)], w) + bias_ref[:, 2 * BLK * p:2 * BLK * (p + 1)])
        t2 = [t2p[h // 2][:, (h % 2) * BLK:(h % 2 + 1) * BLK] for h in heads]
        sp = [_softplus2(t2[h]) for h in heads]
        if masked:
            sp = [jnp.where(vis, s, 0.0) for s in sp]
        sums = [_block_sums(sp[h], tw) for h in heads]
        a = []
        for h in heads:
            later, total = sums[h]
            carry = carry_ref[h]
            a_h = jnp.exp2(t2[h] - sp[h] - later - carry)
            if masked:
                a_h = jnp.where(vis, a_h, 0.0)
            carry_ref[h] = carry + total
            a.append(a_h.astype(BF16))
        for p in pairs:
            vp = vb_ref[j, :, pair_cols(p)]
            vbd = jnp.concatenate([jnp.where(lane < SB_HEAD_DIM, vp, 0),
                                   jnp.where(lane >= SB_HEAD_DIM, vp, 0)], axis=0)
            acc_ref[p] += _dot(jnp.concatenate([a[2 * p], a[2 * p + 1]], axis=1), vbd)

    for d in range(sub - 1, -1, -1):
        step(i * sub + d, True)

    def body(jj, _):
        step(i * sub - 1 - jj, False)
        return 0

    lax.fori_loop(0, i * sub, body, 0)
    for p in range(SB_HEADS // 2):
        o_ref[:, 2 * SB_HEAD_DIM * p:2 * SB_HEAD_DIM * (p + 1)] = acc_ref[p]


def _sb_prompt(q, kt, vb, bias2, tw, bq):
    n_kblk = kt.shape[0]
    n_rows = q.shape[0]
    assert n_rows % bq == 0 and n_rows == n_kblk * BLK
    return pl.pallas_call(
        _sb_prompt_kernel,
        grid=(n_rows // bq,),
        in_specs=[
            pl.BlockSpec((bq, SB_W), lambda i: (i, 0)),
            pl.BlockSpec((n_kblk, SB_W, BLK), lambda i: (0, 0, 0), pipeline_mode=pl.Buffered(1)),
            pl.BlockSpec((n_kblk, BLK, SB_W), lambda i: (0, 0, 0), pipeline_mode=pl.Buffered(1)),
            pl.BlockSpec((1, SB_HEADS * BLK), lambda i: (0, 0)),
            pl.BlockSpec((2 * BLK, 2 * BLK), lambda i: (0, 0)),
        ],
        out_specs=pl.BlockSpec((bq, SB_W), lambda i: (i, 0)),
        out_shape=jax.ShapeDtypeStruct((n_rows, SB_W), F32),
        scratch_shapes=[pltpu.VMEM((bq, SB_W), BF16),
                        pltpu.VMEM((SB_HEADS, bq, BLK), F32),
                        pltpu.VMEM((SB_HEADS // 2, bq, 2 * SB_HEAD_DIM), F32)],
        compiler_params=pltpu.CompilerParams(
            dimension_semantics=("arbitrary",), vmem_limit_bytes=VMEM_LIMIT),
        name="sb_prompt",
    )(q, kt, vb, bias2, tw)


def _sb_decode_kernel(t_new, pages_per_step, pt_ref, q_ref, kn_ref, vn_ref, bias_ref, tw_ref, *refs):
    k_refs = refs[:pages_per_step]
    v_refs = refs[pages_per_step:2 * pages_per_step]
    o_ref, qbd_ref, carry_ref, acc_ref = refs[2 * pages_per_step:]
    j = pl.program_id(1)
    rows = SB_HEADS * t_new
    tw = tw_ref[...]
    bias = bias_ref[...]
    head_of_lane = lax.broadcasted_iota(jnp.int32, (t_new, SB_W), 1) // SB_HEAD_DIM

    @pl.when(j == 0)
    def _():
        q = q_ref[0]
        qbd_ref[...] = jnp.concatenate(
            [jnp.where(head_of_lane == h, q, 0.0) for h in range(SB_HEADS)], axis=0).astype(BF16)
        pad_rows = jnp.zeros((BLK - t_new, SB_W), F32)
        knt = jnp.concatenate([kn_ref[0], pad_rows], axis=0).T.astype(BF16)
        vnt = jnp.concatenate([vn_ref[0], pad_rows], axis=0).T.astype(BF16)
        tq = lax.broadcasted_iota(jnp.int32, (rows, BLK), 0) % t_new
        ks = lax.broadcasted_iota(jnp.int32, (rows, BLK), 1)
        t2 = _dot(qbd_ref[...], knt) + bias
        a, c = _sb_tile(t2, ks < tq, jnp.zeros((rows, BLK), F32), tw)
        carry_ref[...] = c
        acc_ref[...] = _dot_nt(a.astype(BF16), vnt)

    qbd = qbd_ref[...]
    t2s = [_dot(qbd, k_refs[r][0].astype(BF16)) + bias for r in range(pages_per_step)]
    sps = [_softplus2(t2) for t2 in t2s]
    later, total = _block_sums(jnp.concatenate(sps, axis=0), tw)
    carry = carry_ref[...]
    acc = acc_ref[...]
    for r in range(pages_per_step):
        rs = slice(r * rows, (r + 1) * rows)
        a = jnp.exp2(t2s[r] - sps[r] - later[rs] - carry)
        carry = carry + total[rs]
        acc = acc + _dot_nt(a.astype(BF16), v_refs[r][0].astype(BF16))
    carry_ref[...] = carry
    acc_ref[...] = acc

    @pl.when(j == pl.num_programs(1) - 1)
    def _():
        out = jnp.zeros((t_new, SB_W), F32)
        for h in range(SB_HEADS):
            out = out + jnp.where(head_of_lane == h, acc[h * t_new:(h + 1) * t_new], 0.0)
        o_ref[0] = out


def _sb_decode(q, k_new, v_new, cache_kt, cache_vt, page_table, bias_rows, tw, pages_per_step):
    b, t_new, _ = q.shape
    n_pages = page_table.shape[1]
    page = cache_kt.shape[2]
    assert page == BLK and n_pages % pages_per_step == 0
    n_steps = n_pages // pages_per_step
    rows = SB_HEADS * t_new

    def page_spec(r):
        return pl.BlockSpec(
            (1, SB_W, page),
            lambda bi, j, pt: (pt[bi, n_pages - 1 - (j * pages_per_step + r)], 0, 0))

    seq = pl.BlockSpec((1, t_new, SB_W), lambda bi, j, pt: (bi, 0, 0))
    grid_spec = pltpu.PrefetchScalarGridSpec(
        num_scalar_prefetch=1,
        grid=(b, n_steps),
        in_specs=[seq, seq, seq,
                  pl.BlockSpec((rows, BLK), lambda bi, j, pt: (0, 0)),
                  pl.BlockSpec((2 * BLK, 2 * BLK), lambda bi, j, pt: (0, 0))]
        + [page_spec(r) for r in range(pages_per_step)]
        + [page_spec(r) for r in range(pages_per_step)],
        out_specs=seq,
        scratch_shapes=[pltpu.VMEM((rows, SB_W), BF16),
                        pltpu.VMEM((rows, BLK), F32),
                        pltpu.VMEM((rows, SB_W), F32)],
    )
    return pl.pallas_call(
        functools.partial(_sb_decode_kernel, t_new, pages_per_step),
        grid_spec=grid_spec,
        out_shape=jax.ShapeDtypeStruct((b, t_new, SB_W), F32),
        compiler_params=pltpu.CompilerParams(
            dimension_semantics=("arbitrary", "arbitrary"), vmem_limit_bytes=VMEM_LIMIT),
        name="sb_decode",
    )(page_table, q, k_new, v_new, bias_rows, tw,
      *([cache_kt] * pages_per_step), *([cache_vt] * pages_per_step))


def _merge_kernel(alpha, shift, *refs):
    x_ref = refs[0]
    n_mix = 4
    if shift:
        mixer = [jnp.concatenate([refs[1 + 2 * m][shift:, :], refs[2 + 2 * m][...]], axis=0)
                 for m in range(n_mix)]
        refs = refs[1 + 2 * n_mix:]
    else:
        mixer = [refs[1 + m][...] for m in range(n_mix)]
        refs = refs[1 + n_mix:]
    og, zg, osb, zs = mixer
    lig_ref, lib_ref, ng_ref, w_ref, log_ref, lob_ref, y_ref = refs
    h = _layernorm(x_ref[...], lig_ref[...], lib_ref[...])
    parts = []
    for hd in range(GDN_HEADS):
        sl = slice(hd * GDN_DV, (hd + 1) * GDN_DV)
        o = og[:, sl]
        o = o * lax.rsqrt(jnp.mean(o * o, axis=-1, keepdims=True) + RMS_EPS) * ng_ref[...]
        z = zg[:, sl]
        parts.append(o * (z * _sigmoid(z)))
    parts.append(osb * (zs * _sigmoid(zs)))
    cat = jnp.concatenate(parts, axis=-1).astype(BF16)
    mix = _dot(cat, w_ref[...])
    y_ref[...] = _layernorm(alpha * h + mix, log_ref[...], lob_ref[...])


def _merge(x, mixer, ln_in_g, ln_in_b, norm_g, w_out, ln_out_g, ln_out_b, shift, alpha):
    n = x.shape[0]
    n_blk = n // BLK
    assert BLK % shift == 0 if shift else True
    vec = pl.BlockSpec((1, D_MODEL), lambda i: (0, 0))
    in_specs = [pl.BlockSpec((BLK, D_MODEL), lambda i: (i, 0))]
    args = [x]
    for m in mixer:
        in_specs.append(pl.BlockSpec((BLK, m.shape[1]), lambda i: (i, 0)))
        args.append(m)
        if shift:
            per_blk = BLK // shift
            in_specs.append(pl.BlockSpec((shift, m.shape[1]), lambda i: ((i + 1) * per_blk, 0)))
            args.append(m)
    in_specs += [vec, vec, pl.BlockSpec((1, GDN_DV), lambda i: (0, 0)),
                 pl.BlockSpec((D_MODEL, D_MODEL), lambda i: (0, 0)), vec, vec]
    args += [ln_in_g, ln_in_b, norm_g, w_out, ln_out_g, ln_out_b]
    return pl.pallas_call(
        functools.partial(_merge_kernel, alpha, shift),
        grid=(n_blk,),
        in_specs=in_specs,
        out_specs=pl.BlockSpec((BLK, D_MODEL), lambda i: (i, 0)),
        out_shape=jax.ShapeDtypeStruct((n, D_MODEL), F32),
        compiler_params=pltpu.CompilerParams(
            dimension_semantics=("arbitrary",), vmem_limit_bytes=VMEM_LIMIT),
        name="merge",
    )(*args)


def _reorder_w_in(w_in):
    offs = [0]
    for s in PROJ_SIZES:
        offs.append(offs[-1] + s)
    c, zg, a, b, q, k, v, zs = [w_in[:, offs[i]:offs[i + 1]] for i in range(len(PROJ_SIZES))]
    pad = jnp.zeros((w_in.shape[0], LANES - 2 * GDN_HEADS), w_in.dtype)
    return jnp.concatenate([c, zg, q, k, v, zs, a, b, pad], axis=1).astype(BF16)


def _lane_row(v, width=LANES):
    return jnp.pad(v.astype(F32), (0, width - v.shape[0])).reshape(1, width)


def kernel(x_prompt, x_sample, cache_k, cache_v, page_table, state_gdn, state_conv, meta_tokens,
           ln_in_g, ln_in_b, w_in, conv_w, a_log, dt_bias, gdn_norm_g, sb_bias, w_out, ln_out_g,
           ln_out_b):
    depth = w_in.shape[0]
    assert depth == 1, "single-layer trunk only"
    batch, seq, _ = x_prompt.shape
    dec_batch, dec_seq, _ = x_sample.shape
    assert batch == 1 and seq % BLK == 0 and meta_tokens.shape[0] == N_META
    assert (dec_batch * dec_seq) % BLK == 0 and dec_seq <= GDN_CHUNK
    alpha = (2 * depth) ** 0.25
    length = N_META + seq
    n_pad = -(-length // SB_BQ) * SB_BQ
    q_scale = LOG2E * SB_HEAD_DIM ** -0.5

    lig = ln_in_g.reshape(1, D_MODEL)
    lib = ln_in_b.reshape(1, D_MODEL)
    w = _reorder_w_in(w_in[0])
    w_o = w_out[0].astype(BF16)
    cw = jnp.pad(conv_w[0], ((0, SUBLANES - CONV_W), (0, 0)))
    alog = _lane_row(a_log[0])
    dtb = _lane_row(dt_bias[0])
    norm_g = gdn_norm_g[0].reshape(1, GDN_DV)
    log_ = ln_out_g[0].reshape(1, D_MODEL)
    lob = ln_out_b[0].reshape(1, D_MODEL)
    tw = _cumsum_weights()
    bias = sb_bias[0].astype(F32) * LOG2E

    xp = x_prompt[0]
    c_p, zg_p, ab_p, q_p, zs_p, kt32_p, vt32_p, kt_p, vb_p = _project(
        xp, meta_tokens.astype(F32), lig, lib, w, q_scale, n_pad // BLK)
    og_p, s_p = _gdn(c_p, ab_p, cw, alog, dtb,
                     jnp.zeros((1, GDN_HEADS, GDN_DK, GDN_DV), F32),
                     jnp.zeros((1, SUBLANES, CONV_DIM), F32), n_pad, length, GDN_CHUNKS_PER_STEP)
    os_p = _sb_prompt(q_p, kt_p, vb_p, jnp.repeat(bias, BLK).reshape(1, SB_HEADS * BLK), tw, SB_BQ)
    y_p = _merge(xp, (og_p, zg_p, os_p, zs_p), lig, lib, norm_g, w_o, log_, lob, N_META, alpha)

    xs = x_sample.reshape(dec_batch * dec_seq, D_MODEL)
    c_s, zg_s, ab_s, q_s, zs_s, k_s, v_s = _project(xs, None, lig, lib, w, q_scale,
                                                     dec_batch * dec_seq // BLK)

    def per_seq_chunk(a):
        a = a.reshape(dec_batch, dec_seq, a.shape[-1])
        a = jnp.pad(a, ((0, 0), (0, GDN_CHUNK - dec_seq), (0, 0)))
        return a.reshape(dec_batch * GDN_CHUNK, a.shape[-1])

    conv0 = jnp.pad(state_conv[0].astype(F32), ((0, 0), (SUBLANES - (CONV_W - 1), 0), (0, 0)))
    og_s, s_s = _gdn(per_seq_chunk(c_s), per_seq_chunk(ab_s), cw, alog, dtb,
                     state_gdn[0].astype(F32), conv0, GDN_CHUNK, dec_seq, 1)
    og_s = og_s.reshape(dec_batch, GDN_CHUNK, GDN_VW)[:, :dec_seq].reshape(dec_batch * dec_seq, GDN_VW)
    n_phys, page = cache_k.shape[1], cache_k.shape[2]

    def pages_t(cache):
        return jnp.transpose(cache, (0, 2, 3, 1)).reshape(n_phys, SB_W, page)

    os_s = _sb_decode(q_s.reshape(dec_batch, dec_seq, SB_W), k_s.reshape(dec_batch, dec_seq, SB_W),
                      v_s.reshape(dec_batch, dec_seq, SB_W),
                      pages_t(cache_k[0]), pages_t(cache_v[0]), page_table,
                      jnp.broadcast_to(jnp.repeat(bias, dec_seq)[:, None], (SB_HEADS * dec_seq, BLK)),
                      tw, DECODE_PAGES_PER_STEP)
    y_s = _merge(xs, (og_s, zg_s, os_s.reshape(dec_batch * dec_seq, SB_W), zs_s), lig, lib, norm_g,
                 w_o, log_, lob, 0, alpha)

    def heads_last(t):
        return jnp.transpose(t.reshape(SB_HEADS, SB_HEAD_DIM, length), (2, 0, 1))[None, None]

    kv_s = (1, dec_batch, dec_seq, SB_HEADS, SB_HEAD_DIM)
    c_s3 = c_s.reshape(dec_batch, dec_seq, CONV_DIM)
    conv_s = jnp.concatenate([state_conv[0].astype(F32), c_s3], axis=1)[:, -(CONV_W - 1):]
    return (y_p[None], y_s.reshape(dec_batch, dec_seq, D_MODEL),
            heads_last(kt32_p), heads_last(vt32_p),
            k_s.reshape(kv_s), v_s.reshape(kv_s),
            s_p[None], s_s[None],
            c_p[length - (CONV_W - 1):length].reshape(1, 1, CONV_W - 1, CONV_DIM), conv_s[None])
```

```python
import functools
import math

import jax
import jax.numpy as jnp
from jax import lax
from jax.experimental import pallas as pl
from jax.experimental.pallas import tpu as pltpu

F32 = jnp.float32
BF16 = jnp.bfloat16

D_MODEL = 1024
N_META = 16
GDN_HEADS = 4
GDN_DK = 128
GDN_DV = 128
GDN_CHUNK = 64
CONV_W = 4
SB_HEADS = 8
SB_HEAD_DIM = 64
GDN_QK = GDN_HEADS * GDN_DK
GDN_VW = GDN_HEADS * GDN_DV
CONV_DIM = 2 * GDN_QK + GDN_VW
SB_W = SB_HEADS * SB_HEAD_DIM
PROJ_SIZES = (CONV_DIM, GDN_VW, GDN_HEADS, GDN_HEADS, SB_W, SB_W, SB_W, SB_W)
LN_EPS = 1e-5
RMS_EPS = 1e-6
L2_EPS = 1e-6
LOG2E = 1.4426950408889634

LANES = 128
SUBLANES = 8
BLK = 128
SB_BQ = 256
DECODE_PAGES_PER_STEP = 16
MERGE_ROWS = 256
GDN_CHUNKS_PER_STEP = 4
VMEM_LIMIT = 52 * 1024 * 1024
COL_C = 0
COL_ZG = COL_C + CONV_DIM
COL_Q = COL_ZG + GDN_VW
COL_K = COL_Q + SB_W
COL_V = COL_K + SB_W
COL_ZS = COL_V + SB_W
COL_AB = COL_ZS + SB_W
PROJ_COLS = COL_AB + LANES


def _dot(a, b):
    return jnp.dot(a, b, preferred_element_type=F32)


def _dot_nt(a, b):
    return lax.dot_general(a, b, (((1,), (1,)), ((), ())), preferred_element_type=F32)


def _split2(x):
    hi = x.astype(BF16)
    return hi, (x - hi.astype(F32)).astype(BF16)


def _dot1(a, b):
    return _dot(a.astype(BF16), b.astype(BF16))


def _dot1_nt(a, b):
    return _dot_nt(a.astype(BF16), b.astype(BF16))


def _dot3(a, b):
    a_hi, a_lo = _split2(a)
    b_hi, b_lo = _split2(b)
    return _dot(a_hi, b_hi) + (_dot(a_hi, b_lo) + _dot(a_lo, b_hi))


def _layernorm(x, g, b):
    mu = jnp.mean(x, axis=-1, keepdims=True)
    xc = x - mu
    var = jnp.mean(xc * xc, axis=-1, keepdims=True)
    return xc * lax.rsqrt(var + LN_EPS) * g + b


def _sigmoid(x):
    return 1.0 / (1.0 + jnp.exp(-x))


def _softplus(x):
    return jnp.maximum(x, 0.0) + jnp.log(1.0 + jnp.exp(-jnp.abs(x)))


def _project_kernel(n_head, q_scale, *refs):
    if n_head:
        x_ref, head_ref, g_ref, b_ref, w_ref = refs[:5]
        (c_ref, zg_ref, ab_ref, q_ref, zs_ref, kt32_ref, vt32_ref, kt_ref, vb_ref,
         tail_ref, kv_ref) = refs[5:]

        @pl.when(pl.program_id(0) == 0)
        def _():
            tail_ref[...] = head_ref[...]

        xb = x_ref[...]
        x = jnp.concatenate([tail_ref[...], xb[:BLK - n_head]], axis=0)
        tail_ref[...] = xb[BLK - n_head:]
    else:
        x_ref, g_ref, b_ref, w_ref = refs[:4]
        c_ref, zg_ref, ab_ref, q_ref, zs_ref, k_ref, v_ref = refs[4:]
        x = x_ref[...]
    h = _layernorm(x, g_ref[...], b_ref[...]).astype(BF16)

    def proj(col, width):
        return _dot(h, w_ref[:, col:col + width])

    c_ref[...] = proj(COL_C, CONV_DIM)
    zg_ref[...] = proj(COL_ZG, GDN_VW)
    ab_ref[...] = proj(COL_AB, LANES)
    q_ref[...] = proj(COL_Q, SB_W) * q_scale
    zs_ref[...] = proj(COL_ZS, SB_W)
    k = proj(COL_K, SB_W)
    v = proj(COL_V, SB_W)
    if n_head:
        kv_ref[0] = k
        kv_ref[1] = v
        kt = kv_ref[0].T
        kt32_ref[...] = kt
        vt32_ref[...] = kv_ref[1].T
        kt_ref[0] = kt.astype(BF16)
        vb_ref[0] = v.astype(BF16)
    else:
        k_ref[...] = k
        v_ref[...] = v


def _project(x, head, ln_g, ln_b, w, q_scale, n_blk):
    n_head = 0 if head is None else head.shape[0]
    n_x_blk = x.shape[0] // BLK
    n = n_blk * BLK
    length = x.shape[0] + n_head
    last_t = (length - 1) // BLK

    def row(width):
        return pl.BlockSpec((BLK, width), lambda i: (i, 0))

    def const(shape):
        return pl.BlockSpec(shape, lambda i: (0,) * len(shape))

    in_specs = [pl.BlockSpec((BLK, D_MODEL), lambda i: (jnp.minimum(i, n_x_blk - 1), 0))]
    args = [x]
    if n_head:
        in_specs.append(const((n_head, D_MODEL)))
        args.append(head)
    in_specs += [const((1, D_MODEL)), const((1, D_MODEL)), const((D_MODEL, PROJ_COLS))]
    args += [ln_g, ln_b, w]
    out_shape = [jax.ShapeDtypeStruct((n, width), F32) for width in (CONV_DIM, GDN_VW, LANES, SB_W, SB_W)]
    out_specs = [row(width) for width in (CONV_DIM, GDN_VW, LANES, SB_W, SB_W)]
    scratch = []
    if n_head:
        out_shape += [jax.ShapeDtypeStruct((SB_W, length), F32),
                      jax.ShapeDtypeStruct((SB_W, length), F32),
                      jax.ShapeDtypeStruct((n_blk, SB_W, BLK), BF16),
                      jax.ShapeDtypeStruct((n_blk, BLK, SB_W), BF16)]
        out_specs += [pl.BlockSpec((SB_W, BLK), lambda i: (0, jnp.minimum(i, last_t))),
                      pl.BlockSpec((SB_W, BLK), lambda i: (0, jnp.minimum(i, last_t))),
                      pl.BlockSpec((1, SB_W, BLK), lambda i: (i, 0, 0)),
                      pl.BlockSpec((1, BLK, SB_W), lambda i: (i, 0, 0))]
        scratch = [pltpu.VMEM((n_head, D_MODEL), F32), pltpu.VMEM((2, BLK, SB_W), F32)]
    else:
        out_shape += [jax.ShapeDtypeStruct((n, SB_W), F32)] * 2
        out_specs += [row(SB_W), row(SB_W)]
    return pl.pallas_call(
        functools.partial(_project_kernel, n_head, q_scale),
        grid=(n_blk,),
        in_specs=in_specs,
        out_specs=out_specs,
        out_shape=out_shape,
        scratch_shapes=scratch,
        compiler_params=pltpu.CompilerParams(
            dimension_semantics=("arbitrary",), vmem_limit_bytes=VMEM_LIMIT),
        name="project",
    )(*args)


def _gdn_kernel(chunk, n_sub, hi, c_ref, ab_ref, cw_ref, alog_ref, dt_ref, s0_ref, conv0_ref,
                o_ref, s_ref, xs_ref):
    j = pl.program_id(1)
    C = chunk
    R = C * n_sub

    @pl.when(j == 0)
    def _():
        xs_ref[0:SUBLANES, :] = conv0_ref[0]
        s_ref[...] = s0_ref[...]

    valid = j * R + lax.broadcasted_iota(jnp.int32, (R, 1), 0) < hi

    xs_ref[SUBLANES:SUBLANES + R, :] = jnp.where(valid, c_ref[...], 0.0)
    first = SUBLANES - (CONV_W - 1)
    y = xs_ref[first:first + R, :] * cw_ref[0:1, :]
    for i in range(1, CONV_W):
        y = y + xs_ref[first + i:first + i + R, :] * cw_ref[i:i + 1, :]
    xs_ref[0:SUBLANES, :] = xs_ref[R:R + SUBLANES, :]
    y = y * _sigmoid(y)

    ab = ab_ref[...]
    g_all = jnp.where(valid, -jnp.exp(alog_ref[...]) * _softplus(ab + dt_ref[...]), 0.0)
    beta_all = jnp.where(valid, _sigmoid(ab), 0.0)
    ri = lax.broadcasted_iota(jnp.int32, (C, C), 0)
    ci = lax.broadcasted_iota(jnp.int32, (C, C), 1)
    causal = ri >= ci
    strict = ri > ci
    tri = causal.astype(BF16)
    eye = (ri == ci).astype(F32)

    prep = []
    for cc in range(n_sub):
        rows = slice(cc * C, (cc + 1) * C)
        g = g_all[rows]
        beta = beta_all[rows]
        g_hi = g.astype(BF16)
        g_rest = g - g_hi.astype(F32)
        g_mid, g_lo = _split2(g_rest)
        gc = _dot(tri, g_hi) + (_dot(tri, g_mid) + _dot(tri, g_lo))
        gct = gc.T
        for h in range(GDN_HEADS):
            q = y[rows, h * GDN_DK:(h + 1) * GDN_DK]
            k = y[rows, GDN_QK + h * GDN_DK:GDN_QK + (h + 1) * GDN_DK]
            v = y[rows, 2 * GDN_QK + h * GDN_DV:2 * GDN_QK + (h + 1) * GDN_DV]
            q = q * lax.rsqrt(jnp.sum(q * q, axis=-1, keepdims=True) + L2_EPS) * (GDN_DK ** -0.5)
            k = k * lax.rsqrt(jnp.sum(k * k, axis=-1, keepdims=True) + L2_EPS)
            gcol = gc[:, h:h + 1]
            grow = gct[h:h + 1, :]
            bcol = beta[:, GDN_HEADS + h:GDN_HEADS + h + 1]
            glast = gc[C - 1:C, h:h + 1]
            decay = jnp.exp(jnp.where(causal, gcol - grow, -jnp.inf))
            kb = k * bcol
            eg = jnp.exp(gcol)
            prep.append(dict(
                m=jnp.where(strict, _dot1_nt(kb, k) * decay, 0.0),
                rhs=jnp.concatenate([v * bcol, kb * eg], axis=-1),
                qk=_dot1_nt(q, k) * decay, qd=q * eg,
                kdt=(k * jnp.exp(glast - gcol)).T,
                eg_last=jnp.exp(glast)))

    ps = [-pr["m"] for pr in prep]
    ts = [eye + p for p in ps]
    for _ in range(int(math.log2(C)) - 1):
        ps = [_dot3(p, p) for p in ps]
        ts = [t + _dot3(t, p) for t, p in zip(ts, ps)]
    sols = [_dot3(t, pr["rhs"]) for t, pr in zip(ts, prep)]

    for cc in range(n_sub):
        ids = [cc * GDN_HEADS + h for h in range(GDN_HEADS)]
        s = [s_ref[0, h] for h in range(GDN_HEADS)]
        v_new = [sols[i][:, :GDN_DV] - _dot1(sols[i][:, GDN_DV:], s[h]) for h, i in enumerate(ids)]
        for h, i in enumerate(ids):
            s_ref[0, h] = s[h] * prep[i]["eg_last"] + _dot1(prep[i]["kdt"], v_new[h])
        for h, i in enumerate(ids):
            o_ref[cc * C:(cc + 1) * C, h * GDN_DV:(h + 1) * GDN_DV] = (
                _dot1(prep[i]["qd"], s[h]) + _dot1(prep[i]["qk"], v_new[h]))


def _gdn(c, ab, conv_w, a_log, dt_bias, s0, conv0, rows_per_seq, hi, n_sub):
    b = s0.shape[0]
    rows = GDN_CHUNK * n_sub
    n_steps = rows_per_seq // rows
    return pl.pallas_call(
        functools.partial(_gdn_kernel, GDN_CHUNK, n_sub, hi),
        grid=(b, n_steps),
        in_specs=[
            pl.BlockSpec((rows, CONV_DIM), lambda bi, j: (bi * n_steps + j, 0)),
            pl.BlockSpec((rows, LANES), lambda bi, j: (bi * n_steps + j, 0)),
            pl.BlockSpec((SUBLANES, CONV_DIM), lambda bi, j: (0, 0)),
            pl.BlockSpec((1, LANES), lambda bi, j: (0, 0)),
            pl.BlockSpec((1, LANES), lambda bi, j: (0, 0)),
            pl.BlockSpec((1, GDN_HEADS, GDN_DK, GDN_DV), lambda bi, j: (bi, 0, 0, 0)),
            pl.BlockSpec((1, SUBLANES, CONV_DIM), lambda bi, j: (bi, 0, 0)),
        ],
        out_specs=[
            pl.BlockSpec((rows, GDN_VW), lambda bi, j: (bi * n_steps + j, 0)),
            pl.BlockSpec((1, GDN_HEADS, GDN_DK, GDN_DV), lambda bi, j: (bi, 0, 0, 0)),
        ],
        out_shape=[
            jax.ShapeDtypeStruct((b * rows_per_seq, GDN_VW), F32),
            jax.ShapeDtypeStruct((b, GDN_HEADS, GDN_DK, GDN_DV), F32),
        ],
        scratch_shapes=[pltpu.VMEM((rows + SUBLANES, CONV_DIM), F32)],
        compiler_params=pltpu.CompilerParams(
            dimension_semantics=("arbitrary", "arbitrary"), vmem_limit_bytes=VMEM_LIMIT),
        name="gdn",
    )(c, ab, conv_w, a_log, dt_bias, s0, conv0)


def _cumsum_weights():
    r = jnp.arange(BLK)[:, None]
    c = jnp.arange(2 * BLK)[None, :]
    return jnp.where(c < BLK, r >= c, True).astype(BF16)


def _softplus2(t2):
    neg_abs = lax.bitcast_convert_type(
        lax.bitcast_convert_type(t2, jnp.uint32) | jnp.uint32(0x80000000), F32)
    return jnp.maximum(t2, 0.0) + jnp.log(1.0 + jnp.exp2(neg_abs)) * LOG2E


def _block_sums(sp, tw):
    res = _dot(sp.astype(BF16), tw)
    return res[:, :BLK], res[:, BLK:]


def _sb_tile(t2, vis, carry, tw):
    sp = _softplus2(t2)
    if vis is not None:
        sp = jnp.where(vis, sp, 0.0)
    incl, total = _block_sums(sp, tw)
    a = jnp.exp2(t2 - incl - carry)
    if vis is not None:
        a = jnp.where(vis, a, 0.0)
    return a, carry + total


def _sb_prompt_kernel(q_ref, kt_ref, vb_ref, bias_ref, tw_ref, o_ref,
                      qb_ref, carry_ref, acc_ref, t2_ref, a_ref):
    i = pl.program_id(0)
    bq = q_ref.shape[0]
    sub = bq // BLK
    pairs = range(SB_HEADS // 2)
    heads = range(SB_HEADS)
    tw = tw_ref[...]
    qidx = i * bq + lax.broadcasted_iota(jnp.int32, (bq, BLK), 0)
    kidx = lax.broadcasted_iota(jnp.int32, (bq, BLK), 1)
    lane = lax.broadcasted_iota(jnp.int32, (BLK, 2 * SB_HEAD_DIM), 1)
    zeros_k = jnp.zeros((SB_HEAD_DIM, BLK), BF16)
    qb_ref[...] = q_ref[...].astype(BF16)
    carry_ref[...] = jnp.zeros_like(carry_ref)
    acc_ref[...] = jnp.zeros_like(acc_ref)

    def pair_cols(p):
        return slice(2 * SB_HEAD_DIM * p, 2 * SB_HEAD_DIM * (p + 1))

    def logits(j, p):
        ktp = kt_ref[j, pair_cols(p), :]
        w = jnp.concatenate([
            jnp.concatenate([ktp[:SB_HEAD_DIM], zeros_k], axis=1),
            jnp.concatenate([zeros_k, ktp[SB_HEAD_DIM:]], axis=1)], axis=0)
        return _dot(qb_ref[:, pair_cols(p)], w) + bias_ref[:, 2 * BLK * p:2 * BLK * (p + 1)]

    def block_sums(t2p, vis):
        out = []
        for hh in range(2):
            t2 = t2p[:, hh * BLK:(hh + 1) * BLK]
            sp = _softplus2(t2)
            if vis is not None:
                sp = jnp.where(vis, sp, 0.0)
            out.append((t2, _block_sums(sp, tw)))
        return out

    def weights(p, staged, vis):
        a = []
        for hh, (t2, (incl, total)) in enumerate(staged):
            h = 2 * p + hh
            carry = carry_ref[h]
            a_h = jnp.exp2(t2 - incl - carry)
            if vis is not None:
                a_h = jnp.where(vis, a_h, 0.0)
            carry_ref[h] = carry + total
            a.append(a_h.astype(BF16))
        return jnp.concatenate(a, axis=1)

    def values(j, p, ap):
        vp = vb_ref[j, :, pair_cols(p)]
        vbd = jnp.concatenate([jnp.where(lane < SB_HEAD_DIM, vp, 0),
                               jnp.where(lane >= SB_HEAD_DIM, vp, 0)], axis=0)
        acc_ref[p] += _dot(ap, vbd)

    for d in range(sub - 1, -1, -1):
        jd = i * sub + d
        vis = jd * BLK + kidx < qidx
        staged = [block_sums(logits(jd, p), vis) for p in pairs]
        for p in pairs:
            values(jd, p, weights(p, staged[p], vis))

    n = i * sub
    for p in pairs:
        t2_ref[p] = logits(jnp.maximum(n - 1, 0), p)
    a_ref[...] = jnp.zeros_like(a_ref)

    def body(jj, _):
        j = n - 1 - jj
        j_next = jnp.maximum(j - 1, 0)
        staged = []
        for p in pairs:
            t2_now = t2_ref[p]
            t2_ref[p] = logits(j_next, p)
            staged.append(block_sums(t2_now, None))
            values(j + 1, p, a_ref[p])
        for p in pairs:
            a_ref[p] = weights(p, staged[p], None)
        return 0

    lax.fori_loop(0, n, body, 0)
    for p in pairs:
        values(0, p, a_ref[p])
        o_ref[:, pair_cols(p)] = acc_ref[p]


def _sb_prompt(q, kt, vb, bias2, tw, bq):
    n_kblk = kt.shape[0]
    n_rows = q.shape[0]
    assert n_rows % bq == 0 and n_rows == n_kblk * BLK
    return pl.pallas_call(
        _sb_prompt_kernel,
        grid=(n_rows // bq,),
        in_specs=[
            pl.BlockSpec((bq, SB_W), lambda i: (i, 0)),
            pl.BlockSpec((n_kblk, SB_W, BLK), lambda i: (0, 0, 0), pipeline_mode=pl.Buffered(1)),
            pl.BlockSpec((n_kblk, BLK, SB_W), lambda i: (0, 0, 0), pipeline_mode=pl.Buffered(1)),
            pl.BlockSpec((1, SB_HEADS * BLK), lambda i: (0, 0)),
            pl.BlockSpec((BLK, 2 * BLK), lambda i: (0, 0)),
        ],
        out_specs=pl.BlockSpec((bq, SB_W), lambda i: (i, 0)),
        out_shape=jax.ShapeDtypeStruct((n_rows, SB_W), F32),
        scratch_shapes=[pltpu.VMEM((bq, SB_W), BF16),
                        pltpu.VMEM((SB_HEADS, bq, BLK), F32),
                        pltpu.VMEM((SB_HEADS // 2, bq, 2 * SB_HEAD_DIM), F32),
                        pltpu.VMEM((SB_HEADS // 2, bq, 2 * BLK), F32),
                        pltpu.VMEM((SB_HEADS // 2, bq, 2 * BLK), BF16)],
        compiler_params=pltpu.CompilerParams(
            dimension_semantics=("arbitrary",), vmem_limit_bytes=VMEM_LIMIT),
        name="sb_prompt",
    )(q, kt, vb, bias2, tw)


def _sb_decode_kernel(t_new, pages_per_step, pt_ref, q_ref, kn_ref, vn_ref, bias_ref, tw_ref, *refs):
    k_refs = refs[:pages_per_step]
    v_refs = refs[pages_per_step:2 * pages_per_step]
    o_ref, qbd_ref, carry_ref, acc_ref = refs[2 * pages_per_step:]
    j = pl.program_id(1)
    rows = SB_HEADS * t_new
    tw = tw_ref[...]
    bias = bias_ref[...]
    head_of_lane = lax.broadcasted_iota(jnp.int32, (t_new, SB_W), 1) // SB_HEAD_DIM

    @pl.when(j == 0)
    def _():
        q = q_ref[0]
        qbd_ref[...] = jnp.concatenate(
            [jnp.where(head_of_lane == h, q, 0.0) for h in range(SB_HEADS)], axis=0).astype(BF16)
        pad_rows = jnp.zeros((BLK - t_new, SB_W), F32)
        knt = jnp.concatenate([kn_ref[0], pad_rows], axis=0).T.astype(BF16)
        vnt = jnp.concatenate([vn_ref[0], pad_rows], axis=0).T.astype(BF16)
        tq = lax.broadcasted_iota(jnp.int32, (rows, BLK), 0) % t_new
        ks = lax.broadcasted_iota(jnp.int32, (rows, BLK), 1)
        t2 = _dot(qbd_ref[...], knt) + bias
        a, c = _sb_tile(t2, ks < tq, jnp.zeros((rows, BLK), F32), tw)
        carry_ref[...] = c
        acc_ref[...] = _dot_nt(a.astype(BF16), vnt)

    qbd = qbd_ref[...]
    t2s = [_dot(qbd, k_refs[r][0].astype(BF16)) + bias for r in range(pages_per_step)]
    sps = [_softplus2(t2) for t2 in t2s]
    incl, total = _block_sums(jnp.concatenate(sps, axis=0), tw)
    carry = carry_ref[...]
    acc = acc_ref[...]
    for r in range(pages_per_step):
        rs = slice(r * rows, (r + 1) * rows)
        a = jnp.exp2(t2s[r] - incl[rs] - carry)
        carry = carry + total[rs]
        acc = acc + _dot_nt(a.astype(BF16), v_refs[r][0].astype(BF16))
    carry_ref[...] = carry
    acc_ref[...] = acc

    @pl.when(j == pl.num_programs(1) - 1)
    def _():
        out = jnp.zeros((t_new, SB_W), F32)
        for h in range(SB_HEADS):
            out = out + jnp.where(head_of_lane == h, acc[h * t_new:(h + 1) * t_new], 0.0)
        o_ref[0] = out


def _sb_decode(q, k_new, v_new, cache_kt, cache_vt, page_table, bias_rows, tw, pages_per_step):
    b, t_new, _ = q.shape
    n_pages = page_table.shape[1]
    page = cache_kt.shape[2]
    assert page == BLK and n_pages % pages_per_step == 0
    n_steps = n_pages // pages_per_step
    rows = SB_HEADS * t_new

    def page_spec(r):
        return pl.BlockSpec(
            (1, SB_W, page),
            lambda bi, j, pt: (pt[bi, n_pages - 1 - (j * pages_per_step + r)], 0, 0))

    seq = pl.BlockSpec((1, t_new, SB_W), lambda bi, j, pt: (bi, 0, 0))
    grid_spec = pltpu.PrefetchScalarGridSpec(
        num_scalar_prefetch=1,
        grid=(b, n_steps),
        in_specs=[seq, seq, seq,
                  pl.BlockSpec((rows, BLK), lambda bi, j, pt: (0, 0)),
                  pl.BlockSpec((BLK, 2 * BLK), lambda bi, j, pt: (0, 0))]
        + [page_spec(r) for r in range(pages_per_step)]
        + [page_spec(r) for r in range(pages_per_step)],
        out_specs=seq,
        scratch_shapes=[pltpu.VMEM((rows, SB_W), BF16),
                        pltpu.VMEM((rows, BLK), F32),
                        pltpu.VMEM((rows, SB_W), F32)],
    )
    return pl.pallas_call(
        functools.partial(_sb_decode_kernel, t_new, pages_per_step),
        grid_spec=grid_spec,
        out_shape=jax.ShapeDtypeStruct((b, t_new, SB_W), F32),
        compiler_params=pltpu.CompilerParams(
            dimension_semantics=("arbitrary", "arbitrary"), vmem_limit_bytes=VMEM_LIMIT),
        name="sb_decode",
    )(page_table, q, k_new, v_new, bias_rows, tw,
      *([cache_kt] * pages_per_step), *([cache_vt] * pages_per_step))


def _merge_kernel(alpha, shift, *refs):
    x_ref = refs[0]
    n_mix = 4
    if shift:
        mixer = [jnp.concatenate([refs[1 + 2 * m][shift:, :], refs[2 + 2 * m][...]], axis=0)
                 for m in range(n_mix)]
        refs = refs[1 + 2 * n_mix:]
    else:
        mixer = [refs[1 + m][...] for m in range(n_mix)]
        refs = refs[1 + n_mix:]
    og, zg, osb, zs = mixer
    lig_ref, lib_ref, ng_ref, w_ref, log_ref, lob_ref, y_ref = refs
    h = _layernorm(x_ref[...], lig_ref[...], lib_ref[...])
    parts = []
    for hd in range(GDN_HEADS):
        sl = slice(hd * GDN_DV, (hd + 1) * GDN_DV)
        o = og[:, sl]
        o = o * lax.rsqrt(jnp.mean(o * o, axis=-1, keepdims=True) + RMS_EPS) * ng_ref[...]
        z = zg[:, sl]
        parts.append(o * (z * _sigmoid(z)))
    parts.append(osb * (zs * _sigmoid(zs)))
    cat = jnp.concatenate(parts, axis=-1).astype(BF16)
    mix = _dot(cat, w_ref[...])
    y_ref[...] = _layernorm(alpha * h + mix, log_ref[...], lob_ref[...])


def _merge(x, mixer, ln_in_g, ln_in_b, norm_g, w_out, ln_out_g, ln_out_b, shift, alpha):
    n = x.shape[0]
    rows = MERGE_ROWS
    n_blk = n // rows
    assert n % rows == 0 and (rows % shift == 0 if shift else True)
    vec = pl.BlockSpec((1, D_MODEL), lambda i: (0, 0))
    in_specs = [pl.BlockSpec((rows, D_MODEL), lambda i: (i, 0))]
    args = [x]
    for m in mixer:
        in_specs.append(pl.BlockSpec((rows, m.shape[1]), lambda i: (i, 0)))
        args.append(m)
        if shift:
            per_blk = rows // shift
            in_specs.append(pl.BlockSpec((shift, m.shape[1]), lambda i: ((i + 1) * per_blk, 0)))
            args.append(m)
    in_specs += [vec, vec, pl.BlockSpec((1, GDN_DV), lambda i: (0, 0)),
                 pl.BlockSpec((D_MODEL, D_MODEL), lambda i: (0, 0)), vec, vec]
    args += [ln_in_g, ln_in_b, norm_g, w_out, ln_out_g, ln_out_b]
    return pl.pallas_call(
        functools.partial(_merge_kernel, alpha, shift),
        grid=(n_blk,),
        in_specs=in_specs,
        out_specs=pl.BlockSpec((rows, D_MODEL), lambda i: (i, 0)),
        out_shape=jax.ShapeDtypeStruct((n, D_MODEL), F32),
        compiler_params=pltpu.CompilerParams(
            dimension_semantics=("arbitrary",), vmem_limit_bytes=VMEM_LIMIT),
        name="merge",
    )(*args)


def _reorder_w_in(w_in):
    offs = [0]
    for s in PROJ_SIZES:
        offs.append(offs[-1] + s)
    c, zg, a, b, q, k, v, zs = [w_in[:, offs[i]:offs[i + 1]] for i in range(len(PROJ_SIZES))]
    pad = jnp.zeros((w_in.shape[0], LANES - 2 * GDN_HEADS), w_in.dtype)
    return jnp.concatenate([c, zg, q, k, v, zs, a, b, pad], axis=1).astype(BF16)


def _lane_row(v, width=LANES):
    return jnp.pad(v.astype(F32), (0, width - v.shape[0])).reshape(1, width)


def kernel(x_prompt, x_sample, cache_k, cache_v, page_table, state_gdn, state_conv, meta_tokens,
           ln_in_g, ln_in_b, w_in, conv_w, a_log, dt_bias, gdn_norm_g, sb_bias, w_out, ln_out_g,
           ln_out_b):
    depth = w_in.shape[0]
    assert depth == 1, "single-layer trunk only"
    batch, seq, _ = x_prompt.shape
    dec_batch, dec_seq, _ = x_sample.shape
    assert batch == 1 and seq % BLK == 0 and meta_tokens.shape[0] == N_META
    assert (dec_batch * dec_seq) % BLK == 0 and dec_seq <= GDN_CHUNK
    alpha = (2 * depth) ** 0.25
    length = N_META + seq
    n_pad = -(-length // SB_BQ) * SB_BQ
    q_scale = LOG2E * SB_HEAD_DIM ** -0.5

    lig = ln_in_g.reshape(1, D_MODEL)
    lib = ln_in_b.reshape(1, D_MODEL)
    w = _reorder_w_in(w_in[0])
    w_o = w_out[0].astype(BF16)
    cw = jnp.pad(conv_w[0], ((0, SUBLANES - CONV_W), (0, 0)))
    alog = _lane_row(a_log[0])
    dtb = _lane_row(dt_bias[0])
    norm_g = gdn_norm_g[0].reshape(1, GDN_DV)
    log_ = ln_out_g[0].reshape(1, D_MODEL)
    lob = ln_out_b[0].reshape(1, D_MODEL)
    tw = _cumsum_weights()
    bias = sb_bias[0].astype(F32) * LOG2E

    xp = x_prompt[0]
    c_p, zg_p, ab_p, q_p, zs_p, kt32_p, vt32_p, kt_p, vb_p = _project(
        xp, meta_tokens.astype(F32), lig, lib, w, q_scale, n_pad // BLK)
    og_p, s_p = _gdn(c_p, ab_p, cw, alog, dtb,
                     jnp.zeros((1, GDN_HEADS, GDN_DK, GDN_DV), F32),
                     jnp.zeros((1, SUBLANES, CONV_DIM), F32), n_pad, length, GDN_CHUNKS_PER_STEP)
    os_p = _sb_prompt(q_p, kt_p, vb_p, jnp.repeat(bias, BLK).reshape(1, SB_HEADS * BLK), tw, SB_BQ)
    y_p = _merge(xp, (og_p, zg_p, os_p, zs_p), lig, lib, norm_g, w_o, log_, lob, N_META, alpha)

    xs = x_sample.reshape(dec_batch * dec_seq, D_MODEL)
    c_s, zg_s, ab_s, q_s, zs_s, k_s, v_s = _project(xs, None, lig, lib, w, q_scale,
                                                     dec_batch * dec_seq // BLK)

    def per_seq_chunk(a):
        a = a.reshape(dec_batch, dec_seq, a.shape[-1])
        a = jnp.pad(a, ((0, 0), (0, GDN_CHUNK - dec_seq), (0, 0)))
        return a.reshape(dec_batch * GDN_CHUNK, a.shape[-1])

    conv0 = jnp.pad(state_conv[0].astype(F32), ((0, 0), (SUBLANES - (CONV_W - 1), 0), (0, 0)))
    og_s, s_s = _gdn(per_seq_chunk(c_s), per_seq_chunk(ab_s), cw, alog, dtb,
                     state_gdn[0].astype(F32), conv0, GDN_CHUNK, dec_seq, 1)
    og_s = og_s.reshape(dec_batch, GDN_CHUNK, GDN_VW)[:, :dec_seq].reshape(dec_batch * dec_seq, GDN_VW)
    n_phys, page = cache_k.shape[1], cache_k.shape[2]

    def pages_t(cache):
        return jnp.transpose(cache, (0, 2, 3, 1)).reshape(n_phys, SB_W, page)

    os_s = _sb_decode(q_s.reshape(dec_batch, dec_seq, SB_W), k_s.reshape(dec_batch, dec_seq, SB_W),
                      v_s.reshape(dec_batch, dec_seq, SB_W),
                      pages_t(cache_k[0]), pages_t(cache_v[0]), page_table,
                      jnp.broadcast_to(jnp.repeat(bias, dec_seq)[:, None], (SB_HEADS * dec_seq, BLK)),
                      tw, DECODE_PAGES_PER_STEP)
    y_s = _merge(xs, (og_s, zg_s, os_s.reshape(dec_batch * dec_seq, SB_W), zs_s), lig, lib, norm_g,
                 w_o, log_, lob, 0, alpha)

    def heads_last(t):
        return jnp.transpose(t.reshape(SB_HEADS, SB_HEAD_DIM, length), (2, 0, 1))[None, None]

    kv_s = (1, dec_batch, dec_seq, SB_HEADS, SB_HEAD_DIM)
    c_s3 = c_s.reshape(dec_batch, dec_seq, CONV_DIM)
    conv_s = jnp.concatenate([state_conv[0].astype(F32), c_s3], axis=1)[:, -(CONV_W - 1):]
    return (y_p[None], y_s.reshape(dec_batch, dec_seq, D_MODEL),
            heads_last(kt32_p), heads_last(vt32_p),
            k_s.reshape(kv_s), v_s.reshape(kv_s),
            s_p[None], s_s[None],
            c_p[length - (CONV_W - 1):length].reshape(1, 1, CONV_W - 1, CONV_DIM), conv_s[None])
```

```python
import functools
import math

import jax
import jax.numpy as jnp
from jax import lax
from jax.experimental import pallas as pl
from jax.experimental.pallas import tpu as pltpu

F32 = jnp.float32
BF16 = jnp.bfloat16

D_MODEL = 1024
N_META = 16
GDN_HEADS = 4
GDN_DK = 128
GDN_DV = 128
GDN_CHUNK = 64
CONV_W = 4
SB_HEADS = 8
SB_HEAD_DIM = 64
GDN_QK = GDN_HEADS * GDN_DK
GDN_VW = GDN_HEADS * GDN_DV
CONV_DIM = 2 * GDN_QK + GDN_VW
SB_W = SB_HEADS * SB_HEAD_DIM
PROJ_SIZES = (CONV_DIM, GDN_VW, GDN_HEADS, GDN_HEADS, SB_W, SB_W, SB_W, SB_W)
LN_EPS = 1e-5
RMS_EPS = 1e-6
L2_EPS = 1e-6
LOG2E = 1.4426950408889634

LANES = 128
SUBLANES = 8
BLK = 128
SB_BQ = 256
DECODE_PAGES_PER_STEP = 32
MERGE_ROWS = 256
PROJ_ROWS = 256
GDN_CHUNKS_PER_STEP = 4
VMEM_LIMIT = 52 * 1024 * 1024
COL_C = 0
COL_ZG = COL_C + CONV_DIM
COL_Q = COL_ZG + GDN_VW
COL_K = COL_Q + SB_W
COL_V = COL_K + SB_W
COL_ZS = COL_V + SB_W
COL_AB = COL_ZS + SB_W
PROJ_COLS = COL_AB + LANES


def _dot(a, b):
    return jnp.dot(a, b, preferred_element_type=F32)


def _dot_nt(a, b):
    return lax.dot_general(a, b, (((1,), (1,)), ((), ())), preferred_element_type=F32)


def _split2(x):
    hi = x.astype(BF16)
    return hi, (x - hi.astype(F32)).astype(BF16)


def _dot1(a, b):
    return _dot(a.astype(BF16), b.astype(BF16))


def _dot1_nt(a, b):
    return _dot_nt(a.astype(BF16), b.astype(BF16))


def _dot3(a, b):
    a_hi, a_lo = _split2(a)
    b_hi, b_lo = _split2(b)
    return _dot(a_hi, b_hi) + (_dot(a_hi, b_lo) + _dot(a_lo, b_hi))


def _layernorm(x, g, b):
    mu = jnp.mean(x, axis=-1, keepdims=True)
    xc = x - mu
    var = jnp.mean(xc * xc, axis=-1, keepdims=True)
    return xc * lax.rsqrt(var + LN_EPS) * g + b


def _sigmoid(x):
    return 1.0 / (1.0 + jnp.exp(-x))


def _softplus(x):
    return jnp.maximum(x, 0.0) + jnp.log(1.0 + jnp.exp(-jnp.abs(x)))


def _project_kernel(n_head, q_scale, *refs):
    if n_head:
        x_ref, head_ref, g_ref, b_ref, w_ref = refs[:5]
        (c_ref, zg_ref, ab_ref, q_ref, zs_ref, kt32_ref, vt32_ref, qt_ref, kb_ref, vt_ref,
         tail_ref, kv_ref) = refs[5:]

        @pl.when(pl.program_id(0) == 0)
        def _():
            tail_ref[...] = head_ref[...]

        xb = x_ref[...]
        rows = xb.shape[0]
        x = jnp.concatenate([tail_ref[...], xb[:rows - n_head]], axis=0)
        tail_ref[...] = xb[rows - n_head:]
    else:
        x_ref, g_ref, b_ref, w_ref = refs[:4]
        c_ref, zg_ref, ab_ref, q_ref, zs_ref, k_ref, v_ref = refs[4:]
        x = x_ref[...]
    h = _layernorm(x, g_ref[...], b_ref[...]).astype(BF16)

    def proj(col, width):
        return _dot(h, w_ref[:, col:col + width])

    c_ref[...] = proj(COL_C, CONV_DIM)
    zg_ref[...] = proj(COL_ZG, GDN_VW)
    ab_ref[...] = proj(COL_AB, LANES)
    q = proj(COL_Q, SB_W) * q_scale
    q_ref[...] = q
    zs_ref[...] = proj(COL_ZS, SB_W)
    k = proj(COL_K, SB_W)
    v = proj(COL_V, SB_W)
    if n_head:
        kv_ref[0] = k
        kv_ref[1] = v
        kv_ref[2] = q
        kt32_ref[...] = kv_ref[0].T
        vt = kv_ref[1].T
        vt32_ref[...] = vt
        qt = kv_ref[2].T.astype(BF16)
        kb = k.astype(BF16)
        vt = vt.astype(BF16)
        for s in range(qt_ref.shape[0]):
            qt_ref[s] = qt[:, s * BLK:(s + 1) * BLK]
            kb_ref[s] = kb[s * BLK:(s + 1) * BLK]
            vt_ref[s] = vt[:, s * BLK:(s + 1) * BLK]
    else:
        k_ref[...] = k
        v_ref[...] = v


def _project(x, head, ln_g, ln_b, w, q_scale, n_blk):
    n_head = 0 if head is None else head.shape[0]
    rows = PROJ_ROWS
    sub = rows // BLK
    n_x_blk = x.shape[0] // rows
    assert x.shape[0] % rows == 0
    n = n_blk * rows
    length = x.shape[0] + n_head
    last_t = (length - 1) // rows

    def row(width):
        return pl.BlockSpec((rows, width), lambda i: (i, 0))

    def const(shape):
        return pl.BlockSpec(shape, lambda i: (0,) * len(shape))

    in_specs = [pl.BlockSpec((rows, D_MODEL), lambda i: (jnp.minimum(i, n_x_blk - 1), 0))]
    args = [x]
    if n_head:
        in_specs.append(const((n_head, D_MODEL)))
        args.append(head)
    in_specs += [const((1, D_MODEL)), const((1, D_MODEL)), const((D_MODEL, PROJ_COLS))]
    args += [ln_g, ln_b, w]
    out_shape = [jax.ShapeDtypeStruct((n, width), F32) for width in (CONV_DIM, GDN_VW, LANES, SB_W, SB_W)]
    out_specs = [row(width) for width in (CONV_DIM, GDN_VW, LANES, SB_W, SB_W)]
    scratch = []
    if n_head:
        out_shape += [jax.ShapeDtypeStruct((SB_W, length), F32),
                      jax.ShapeDtypeStruct((SB_W, length), F32),
                      jax.ShapeDtypeStruct((n_blk * sub, SB_W, BLK), BF16),
                      jax.ShapeDtypeStruct((n_blk * sub, BLK, SB_W), BF16),
                      jax.ShapeDtypeStruct((n_blk * sub, SB_W, BLK), BF16)]
        out_specs += [pl.BlockSpec((SB_W, rows), lambda i: (0, jnp.minimum(i, last_t))),
                      pl.BlockSpec((SB_W, rows), lambda i: (0, jnp.minimum(i, last_t))),
                      pl.BlockSpec((sub, SB_W, BLK), lambda i: (i, 0, 0)),
                      pl.BlockSpec((sub, BLK, SB_W), lambda i: (i, 0, 0)),
                      pl.BlockSpec((sub, SB_W, BLK), lambda i: (i, 0, 0))]
        scratch = [pltpu.VMEM((n_head, D_MODEL), F32), pltpu.VMEM((3, rows, SB_W), F32)]
    else:
        out_shape += [jax.ShapeDtypeStruct((n, SB_W), F32)] * 2
        out_specs += [row(SB_W), row(SB_W)]
    return pl.pallas_call(
        functools.partial(_project_kernel, n_head, q_scale),
        grid=(n_blk,),
        in_specs=in_specs,
        out_specs=out_specs,
        out_shape=out_shape,
        scratch_shapes=scratch,
        compiler_params=pltpu.CompilerParams(
            dimension_semantics=("arbitrary",), vmem_limit_bytes=VMEM_LIMIT),
        name="project",
    )(*args)


def _gdn_kernel(chunk, n_sub, hi, c_ref, ab_ref, cw_ref, alog_ref, dt_ref, s0_ref, conv0_ref,
                tri_ref, eye_ref, o_ref, s_ref, xs_ref):
    j = pl.program_id(1)
    C = chunk
    R = C * n_sub

    @pl.when(j == 0)
    def _():
        xs_ref[0:SUBLANES, :] = conv0_ref[0]
        s_ref[...] = s0_ref[...]

    valid = j * R + lax.broadcasted_iota(jnp.int32, (R, 1), 0) < hi

    xs_ref[SUBLANES:SUBLANES + R, :] = jnp.where(valid, c_ref[...], 0.0)
    first = SUBLANES - (CONV_W - 1)
    y = xs_ref[first:first + R, :] * cw_ref[0:1, :]
    for i in range(1, CONV_W):
        y = y + xs_ref[first + i:first + i + R, :] * cw_ref[i:i + 1, :]
    xs_ref[0:SUBLANES, :] = xs_ref[R:R + SUBLANES, :]
    y = y * _sigmoid(y)

    ab = ab_ref[...]
    g_all = jnp.where(valid, -jnp.exp(alog_ref[...]) * _softplus(ab + dt_ref[...]), 0.0)
    beta_all = jnp.where(valid, _sigmoid(ab), 0.0)
    ri = lax.broadcasted_iota(jnp.int32, (C, C), 0)
    ci = lax.broadcasted_iota(jnp.int32, (C, C), 1)
    causal = ri >= ci
    strict = ri > ci
    tri = tri_ref[...]
    eye = eye_ref[...]

    prep = []
    for cc in range(n_sub):
        rows = slice(cc * C, (cc + 1) * C)
        g = g_all[rows]
        beta = beta_all[rows]
        g_hi = g.astype(BF16)
        g_rest = g - g_hi.astype(F32)
        g_mid, g_lo = _split2(g_rest)
        gc = _dot(tri, g_hi) + (_dot(tri, g_mid) + _dot(tri, g_lo))
        gct = gc.T
        for h in range(GDN_HEADS):
            q = y[rows, h * GDN_DK:(h + 1) * GDN_DK]
            k = y[rows, GDN_QK + h * GDN_DK:GDN_QK + (h + 1) * GDN_DK]
            v = y[rows, 2 * GDN_QK + h * GDN_DV:2 * GDN_QK + (h + 1) * GDN_DV]
            q = q * lax.rsqrt(jnp.sum(q * q, axis=-1, keepdims=True) + L2_EPS) * (GDN_DK ** -0.5)
            k = k * lax.rsqrt(jnp.sum(k * k, axis=-1, keepdims=True) + L2_EPS)
            gcol = gc[:, h:h + 1]
            grow = gct[h:h + 1, :]
            bcol = beta[:, GDN_HEADS + h:GDN_HEADS + h + 1]
            glast = gc[C - 1:C, h:h + 1]
            decay = jnp.exp(jnp.where(causal, gcol - grow, -jnp.inf))
            kb = k * bcol
            eg = jnp.exp(gcol)
            prep.append(dict(
                m=jnp.where(strict, _dot1_nt(kb, k) * decay, 0.0),
                rhs=jnp.concatenate([v * bcol, kb * eg], axis=-1),
                qk=_dot1_nt(q, k) * decay, qd=q * eg,
                kdt=(k * jnp.exp(glast - gcol)).T,
                eg_last=jnp.exp(glast)))

    def dot3_split(a_split, b_split):
        (a_hi, a_lo), (b_hi, b_lo) = a_split, b_split
        return _dot(a_hi, b_hi) + (_dot(a_hi, b_lo) + _dot(a_lo, b_hi))

    ps = [-pr["m"] for pr in prep]
    ts = [eye + p for p in ps]
    p_splits = [_split2(p) for p in ps]
    for _ in range(int(math.log2(C)) - 1):
        p_splits = [_split2(dot3_split(sp, sp)) for sp in p_splits]
        ts = [t + dot3_split(_split2(t), sp) for t, sp in zip(ts, p_splits)]
    sols = [_dot3(t, pr["rhs"]) for t, pr in zip(ts, prep)]

    for cc in range(n_sub):
        ids = [cc * GDN_HEADS + h for h in range(GDN_HEADS)]
        s = [s_ref[0, h] for h in range(GDN_HEADS)]
        v_new = [sols[i][:, :GDN_DV] - _dot1(sols[i][:, GDN_DV:], s[h]) for h, i in enumerate(ids)]
        for h, i in enumerate(ids):
            s_ref[0, h] = s[h] * prep[i]["eg_last"] + _dot1(prep[i]["kdt"], v_new[h])
        for h, i in enumerate(ids):
            o_ref[cc * C:(cc + 1) * C, h * GDN_DV:(h + 1) * GDN_DV] = (
                _dot1(prep[i]["qd"], s[h]) + _dot1(prep[i]["qk"], v_new[h]))


def _gdn(c, ab, conv_w, a_log, dt_bias, s0, conv0, rows_per_seq, hi, n_sub):
    b = s0.shape[0]
    rows = GDN_CHUNK * n_sub
    n_steps = rows_per_seq // rows
    return pl.pallas_call(
        functools.partial(_gdn_kernel, GDN_CHUNK, n_sub, hi),
        grid=(b, n_steps),
        in_specs=[
            pl.BlockSpec((rows, CONV_DIM), lambda bi, j: (bi * n_steps + j, 0)),
            pl.BlockSpec((rows, LANES), lambda bi, j: (bi * n_steps + j, 0)),
            pl.BlockSpec((SUBLANES, CONV_DIM), lambda bi, j: (0, 0)),
            pl.BlockSpec((1, LANES), lambda bi, j: (0, 0)),
            pl.BlockSpec((1, LANES), lambda bi, j: (0, 0)),
            pl.BlockSpec((1, GDN_HEADS, GDN_DK, GDN_DV), lambda bi, j: (bi, 0, 0, 0)),
            pl.BlockSpec((1, SUBLANES, CONV_DIM), lambda bi, j: (bi, 0, 0)),
            pl.BlockSpec((GDN_CHUNK, GDN_CHUNK), lambda bi, j: (0, 0)),
            pl.BlockSpec((GDN_CHUNK, GDN_CHUNK), lambda bi, j: (0, 0)),
        ],
        out_specs=[
            pl.BlockSpec((rows, GDN_VW), lambda bi, j: (bi * n_steps + j, 0)),
            pl.BlockSpec((1, GDN_HEADS, GDN_DK, GDN_DV), lambda bi, j: (bi, 0, 0, 0)),
        ],
        out_shape=[
            jax.ShapeDtypeStruct((b * rows_per_seq, GDN_VW), F32),
            jax.ShapeDtypeStruct((b, GDN_HEADS, GDN_DK, GDN_DV), F32),
        ],
        scratch_shapes=[pltpu.VMEM((rows + SUBLANES, CONV_DIM), F32)],
        compiler_params=pltpu.CompilerParams(
            dimension_semantics=("arbitrary", "arbitrary"), vmem_limit_bytes=VMEM_LIMIT),
        name="gdn",
    )(c, ab, conv_w, a_log, dt_bias, s0, conv0,
      jnp.tril(jnp.ones((GDN_CHUNK, GDN_CHUNK), BF16)), jnp.eye(GDN_CHUNK, dtype=F32))


def _cumsum_weights():
    r = jnp.arange(BLK)[:, None]
    c = jnp.arange(2 * BLK)[None, :]
    return jnp.where(c < BLK, r >= c, True).astype(BF16)


def _softplus2(t2):
    neg_abs = lax.bitcast_convert_type(
        lax.bitcast_convert_type(t2, jnp.uint32) | jnp.uint32(0x80000000), F32)
    return jnp.maximum(t2, 0.0) + jnp.log(1.0 + jnp.exp2(neg_abs)) * LOG2E


def _block_sums(sp, tw):
    res = _dot(sp.astype(BF16), tw)
    return res[:, :BLK], res[:, BLK:]


def _sb_tile(t2, vis, carry, tw):
    sp = _softplus2(t2)
    if vis is not None:
        sp = jnp.where(vis, sp, 0.0)
    incl, total = _block_sums(sp, tw)
    a = jnp.exp2(t2 - incl - carry)
    if vis is not None:
        a = jnp.where(vis, a, 0.0)
    return a, carry + total


def _sb_prompt_kernel(bias_ref, qt_ref, kb_ref, vt_ref, u_ref, o_ref,
                      wq_ref, carry_ref, acc_ref, t2_ref, a_ref):
    i = pl.program_id(0)
    sub = qt_ref.shape[0]
    bq = sub * BLK
    pairs = range(SB_HEADS // 2)
    u = u_ref[...]
    kidx = lax.broadcasted_iota(jnp.int32, (BLK, bq), 0)
    qidx = i * bq + lax.broadcasted_iota(jnp.int32, (BLK, bq), 1)
    zeros_q = jnp.zeros((SB_HEAD_DIM, bq), BF16)
    zeros_k = jnp.zeros((SB_HEAD_DIM, BLK), BF16)

    def pair_cols(p):
        return slice(2 * SB_HEAD_DIM * p, 2 * SB_HEAD_DIM * (p + 1))

    for p in pairs:
        qtp = jnp.concatenate([qt_ref[s, pair_cols(p), :] for s in range(sub)], axis=1)
        wq_ref[p] = jnp.concatenate([
            jnp.concatenate([qtp[:SB_HEAD_DIM], zeros_q], axis=1),
            jnp.concatenate([zeros_q, qtp[SB_HEAD_DIM:]], axis=1)], axis=0)
    carry_ref[...] = jnp.zeros_like(carry_ref)
    acc_ref[...] = jnp.zeros_like(acc_ref)

    def logits(j, p):
        return _dot(kb_ref[j, :, pair_cols(p)], wq_ref[p])

    def block_sums(lp, p, vis):
        out = []
        for hh in range(2):
            t2 = lp[:, hh * bq:(hh + 1) * bq] + bias_ref[2 * p + hh]
            sp = _softplus2(t2)
            if vis is not None:
                sp = jnp.where(vis, sp, 0.0)
            res = _dot(u, sp.astype(BF16))
            out.append((t2, res[:BLK], res[BLK:BLK + SUBLANES]))
        return out

    def weights(p, staged, vis):
        a = []
        for hh, (t2, incl, total) in enumerate(staged):
            h = 2 * p + hh
            carry = carry_ref[h]
            e = (t2 - incl).reshape(BLK // SUBLANES, SUBLANES, bq) - carry[None]
            a_h = jnp.exp2(e).reshape(BLK, bq)
            if vis is not None:
                a_h = jnp.where(vis, a_h, 0.0)
            carry_ref[h] = carry + total
            a.append(a_h.astype(BF16))
        return jnp.concatenate(a, axis=0)

    def values(j, p, ap):
        vtp = vt_ref[j, pair_cols(p), :]
        vbdt = jnp.concatenate([
            jnp.concatenate([vtp[:SB_HEAD_DIM], zeros_k], axis=1),
            jnp.concatenate([zeros_k, vtp[SB_HEAD_DIM:]], axis=1)], axis=0)
        acc_ref[p] += _dot(vbdt, ap)

    for d in range(sub - 1, -1, -1):
        jd = i * sub + d
        vis = jd * BLK + kidx < qidx
        staged = [block_sums(logits(jd, p), p, vis) for p in pairs]
        for p in pairs:
            values(jd, p, weights(p, staged[p], vis))

    n = i * sub
    for p in pairs:
        t2_ref[p] = logits(jnp.maximum(n - 1, 0), p)
    a_ref[...] = jnp.zeros_like(a_ref)

    def body(jj, _):
        j = n - 1 - jj
        j_next = jnp.maximum(j - 1, 0)
        staged = []
        for p in pairs:
            lp = t2_ref[p]
            t2_ref[p] = logits(j_next, p)
            staged.append(block_sums(lp, p, None))
            values(j + 1, p, a_ref[p])
        for p in pairs:
            a_ref[p] = weights(p, staged[p], None)
        return 0

    lax.fori_loop(0, n, body, 0)
    for p in pairs:
        values(0, p, a_ref[p])
        o_ref[:, pair_cols(p)] = acc_ref[p].T


def _sb_prompt(qt, kb, vt, bias2, u, bq):
    n_kblk = kb.shape[0]
    sub = bq // BLK
    assert n_kblk % sub == 0
    return pl.pallas_call(
        _sb_prompt_kernel,
        grid=(n_kblk // sub,),
        in_specs=[
            pl.BlockSpec(memory_space=pltpu.SMEM),
            pl.BlockSpec((sub, SB_W, BLK), lambda i: (i, 0, 0)),
            pl.BlockSpec((n_kblk, BLK, SB_W), lambda i: (0, 0, 0), pipeline_mode=pl.Buffered(1)),
            pl.BlockSpec((n_kblk, SB_W, BLK), lambda i: (0, 0, 0), pipeline_mode=pl.Buffered(1)),
            pl.BlockSpec((BLK + 2 * SUBLANES, BLK), lambda i: (0, 0)),
        ],
        out_specs=pl.BlockSpec((bq, SB_W), lambda i: (i, 0)),
        out_shape=jax.ShapeDtypeStruct((n_kblk * BLK, SB_W), F32),
        scratch_shapes=[pltpu.VMEM((SB_HEADS // 2, 2 * SB_HEAD_DIM, 2 * bq), BF16),
                        pltpu.VMEM((SB_HEADS, SUBLANES, bq), F32),
                        pltpu.VMEM((SB_HEADS // 2, 2 * SB_HEAD_DIM, bq), F32),
                        pltpu.VMEM((SB_HEADS // 2, BLK, 2 * bq), F32),
                        pltpu.VMEM((SB_HEADS // 2, 2 * BLK, bq), BF16)],
        compiler_params=pltpu.CompilerParams(
            dimension_semantics=("arbitrary",), vmem_limit_bytes=VMEM_LIMIT),
        name="sb_prompt",
    )(bias2, qt, kb, vt, u)


def _sb_decode_kernel(t_new, pages_per_step, pt_ref, q_ref, kn_ref, vn_ref, bias_ref, tw_ref, *refs):
    k_refs = refs[:pages_per_step]
    v_refs = refs[pages_per_step:2 * pages_per_step]
    o_ref, qbd_ref, carry_ref, acc_ref = refs[2 * pages_per_step:]
    j = pl.program_id(1)
    rows = SB_HEADS * t_new
    tw = tw_ref[...]
    bias = bias_ref[...]
    head_of_lane = lax.broadcasted_iota(jnp.int32, (t_new, SB_W), 1) // SB_HEAD_DIM

    @pl.when(j == 0)
    def _():
        q = q_ref[0]
        qbd_ref[...] = jnp.concatenate(
            [jnp.where(head_of_lane == h, q, 0.0) for h in range(SB_HEADS)], axis=0).astype(BF16)
        pad_rows = jnp.zeros((BLK - t_new, SB_W), F32)
        knt = jnp.concatenate([kn_ref[0], pad_rows], axis=0).T.astype(BF16)
        vnt = jnp.concatenate([vn_ref[0], pad_rows], axis=0).T.astype(BF16)
        tq = lax.broadcasted_iota(jnp.int32, (rows, BLK), 0) % t_new
        ks = lax.broadcasted_iota(jnp.int32, (rows, BLK), 1)
        t2 = _dot(qbd_ref[...], knt) + bias
        a, c = _sb_tile(t2, ks < tq, jnp.zeros((rows, BLK), F32), tw)
        carry_ref[...] = c
        acc_ref[...] = _dot_nt(a.astype(BF16), vnt)

    qbd = qbd_ref[...]
    t2s = [_dot(qbd, k_refs[r][0].astype(BF16)) + bias for r in range(pages_per_step)]
    sps = [_softplus2(t2) for t2 in t2s]
    incl, total = _block_sums(jnp.concatenate(sps, axis=0), tw)
    carry = carry_ref[...]
    acc = acc_ref[...]
    for r in range(pages_per_step):
        rs = slice(r * rows, (r + 1) * rows)
        a = jnp.exp2(t2s[r] - incl[rs] - carry)
        carry = carry + total[rs]
        acc = acc + _dot_nt(a.astype(BF16), v_refs[r][0].astype(BF16))
    carry_ref[...] = carry
    acc_ref[...] = acc

    @pl.when(j == pl.num_programs(1) - 1)
    def _():
        out = jnp.zeros((t_new, SB_W), F32)
        for h in range(SB_HEADS):
            out = out + jnp.where(head_of_lane == h, acc[h * t_new:(h + 1) * t_new], 0.0)
        o_ref[0] = out


def _sb_decode(q, k_new, v_new, cache_kt, cache_vt, page_table, bias_rows, tw, pages_per_step):
    b, t_new, _ = q.shape
    n_pages = page_table.shape[1]
    page = cache_kt.shape[2]
    assert page == BLK and n_pages % pages_per_step == 0
    n_steps = n_pages // pages_per_step
    rows = SB_HEADS * t_new

    def page_spec(r):
        return pl.BlockSpec(
            (1, SB_W, page),
            lambda bi, j, pt: (pt[bi, n_pages - 1 - (j * pages_per_step + r)], 0, 0))

    seq = pl.BlockSpec((1, t_new, SB_W), lambda bi, j, pt: (bi, 0, 0))
    grid_spec = pltpu.PrefetchScalarGridSpec(
        num_scalar_prefetch=1,
        grid=(b, n_steps),
        in_specs=[seq, seq, seq,
                  pl.BlockSpec((rows, BLK), lambda bi, j, pt: (0, 0)),
                  pl.BlockSpec((BLK, 2 * BLK), lambda bi, j, pt: (0, 0))]
        + [page_spec(r) for r in range(pages_per_step)]
        + [page_spec(r) for r in range(pages_per_step)],
        out_specs=seq,
        scratch_shapes=[pltpu.VMEM((rows, SB_W), BF16),
                        pltpu.VMEM((rows, BLK), F32),
                        pltpu.VMEM((rows, SB_W), F32)],
    )
    return pl.pallas_call(
        functools.partial(_sb_decode_kernel, t_new, pages_per_step),
        grid_spec=grid_spec,
        out_shape=jax.ShapeDtypeStruct((b, t_new, SB_W), F32),
        compiler_params=pltpu.CompilerParams(
            dimension_semantics=("arbitrary", "arbitrary"), vmem_limit_bytes=VMEM_LIMIT),
        name="sb_decode",
    )(page_table, q, k_new, v_new, bias_rows, tw,
      *([cache_kt] * pages_per_step), *([cache_vt] * pages_per_step))


def _merge_kernel(alpha, shift, *refs):
    x_ref = refs[0]
    n_mix = 4
    if shift:
        mixer = [jnp.concatenate([refs[1 + 2 * m][shift:, :], refs[2 + 2 * m][...]], axis=0)
                 for m in range(n_mix)]
        refs = refs[1 + 2 * n_mix:]
    else:
        mixer = [refs[1 + m][...] for m in range(n_mix)]
        refs = refs[1 + n_mix:]
    og, zg, osb, zs = mixer
    lig_ref, lib_ref, ng_ref, w_ref, log_ref, lob_ref, y_ref = refs
    h = _layernorm(x_ref[...], lig_ref[...], lib_ref[...])
    parts = []
    for hd in range(GDN_HEADS):
        sl = slice(hd * GDN_DV, (hd + 1) * GDN_DV)
        o = og[:, sl]
        o = o * lax.rsqrt(jnp.mean(o * o, axis=-1, keepdims=True) + RMS_EPS) * ng_ref[...]
        z = zg[:, sl]
        parts.append(o * (z * _sigmoid(z)))
    parts.append(osb * (zs * _sigmoid(zs)))
    cat = jnp.concatenate(parts, axis=-1).astype(BF16)
    mix = _dot(cat, w_ref[...])
    y_ref[...] = _layernorm(alpha * h + mix, log_ref[...], lob_ref[...])


def _merge(x, mixer, ln_in_g, ln_in_b, norm_g, w_out, ln_out_g, ln_out_b, shift, alpha):
    n = x.shape[0]
    rows = MERGE_ROWS
    n_blk = n // rows
    assert n % rows == 0 and (rows % shift == 0 if shift else True)
    vec = pl.BlockSpec((1, D_MODEL), lambda i: (0, 0))
    in_specs = [pl.BlockSpec((rows, D_MODEL), lambda i: (i, 0))]
    args = [x]
    for m in mixer:
        in_specs.append(pl.BlockSpec((rows, m.shape[1]), lambda i: (i, 0)))
        args.append(m)
        if shift:
            per_blk = rows // shift
            in_specs.append(pl.BlockSpec((shift, m.shape[1]), lambda i: ((i + 1) * per_blk, 0)))
            args.append(m)
    in_specs += [vec, vec, pl.BlockSpec((1, GDN_DV), lambda i: (0, 0)),
                 pl.BlockSpec((D_MODEL, D_MODEL), lambda i: (0, 0)), vec, vec]
    args += [ln_in_g, ln_in_b, norm_g, w_out, ln_out_g, ln_out_b]
    return pl.pallas_call(
        functools.partial(_merge_kernel, alpha, shift),
        grid=(n_blk,),
        in_specs=in_specs,
        out_specs=pl.BlockSpec((rows, D_MODEL), lambda i: (i, 0)),
        out_shape=jax.ShapeDtypeStruct((n, D_MODEL), F32),
        compiler_params=pltpu.CompilerParams(
            dimension_semantics=("arbitrary",), vmem_limit_bytes=VMEM_LIMIT),
        name="merge",
    )(*args)


def _reorder_w_in(w_in):
    offs = [0]
    for s in PROJ_SIZES:
        offs.append(offs[-1] + s)
    c, zg, a, b, q, k, v, zs = [w_in[:, offs[i]:offs[i + 1]] for i in range(len(PROJ_SIZES))]
    pad = jnp.zeros((w_in.shape[0], LANES - 2 * GDN_HEADS), w_in.dtype)
    return jnp.concatenate([c, zg, q, k, v, zs, a, b, pad], axis=1).astype(BF16)


def _lane_row(v, width=LANES):
    return jnp.pad(v.astype(F32), (0, width - v.shape[0])).reshape(1, width)


def kernel(x_prompt, x_sample, cache_k, cache_v, page_table, state_gdn, state_conv, meta_tokens,
           ln_in_g, ln_in_b, w_in, conv_w, a_log, dt_bias, gdn_norm_g, sb_bias, w_out, ln_out_g,
           ln_out_b):
    depth = w_in.shape[0]
    assert depth == 1, "single-layer trunk only"
    batch, seq, _ = x_prompt.shape
    dec_batch, dec_seq, _ = x_sample.shape
    assert batch == 1 and seq % BLK == 0 and meta_tokens.shape[0] == N_META
    assert (dec_batch * dec_seq) % BLK == 0 and dec_seq <= GDN_CHUNK
    alpha = (2 * depth) ** 0.25
    length = N_META + seq
    n_pad = -(-length // SB_BQ) * SB_BQ
    q_scale = LOG2E * SB_HEAD_DIM ** -0.5

    lig = ln_in_g.reshape(1, D_MODEL)
    lib = ln_in_b.reshape(1, D_MODEL)
    w = _reorder_w_in(w_in[0])
    w_o = w_out[0].astype(BF16)
    cw = jnp.pad(conv_w[0], ((0, SUBLANES - CONV_W), (0, 0)))
    alog = _lane_row(a_log[0])
    dtb = _lane_row(dt_bias[0])
    norm_g = gdn_norm_g[0].reshape(1, GDN_DV)
    log_ = ln_out_g[0].reshape(1, D_MODEL)
    lob = ln_out_b[0].reshape(1, D_MODEL)
    tw = _cumsum_weights()
    bias = sb_bias[0].astype(F32) * LOG2E

    xp = x_prompt[0]
    c_p, zg_p, ab_p, _, zs_p, kt32_p, vt32_p, qt_p, kb_p, vt_p = _project(
        xp, meta_tokens.astype(F32), lig, lib, w, q_scale, n_pad // PROJ_ROWS)
    og_p, s_p = _gdn(c_p, ab_p, cw, alog, dtb,
                     jnp.zeros((1, GDN_HEADS, GDN_DK, GDN_DV), F32),
                     jnp.zeros((1, SUBLANES, CONV_DIM), F32), n_pad, length, GDN_CHUNKS_PER_STEP)
    idx = jnp.arange(BLK)
    u = jnp.concatenate([idx[None, :] >= idx[:, None], jnp.ones((2 * SUBLANES, BLK), bool)]).astype(BF16)
    os_p = _sb_prompt(qt_p, kb_p, vt_p, bias, u, SB_BQ)
    y_p = _merge(xp, (og_p, zg_p, os_p, zs_p), lig, lib, norm_g, w_o, log_, lob, N_META, alpha)

    xs = x_sample.reshape(dec_batch * dec_seq, D_MODEL)
    c_s, zg_s, ab_s, q_s, zs_s, k_s, v_s = _project(xs, None, lig, lib, w, q_scale,
                                                     dec_batch * dec_seq // PROJ_ROWS)

    def per_seq_chunk(a):
        a = a.reshape(dec_batch, dec_seq, a.shape[-1])
        a = jnp.pad(a, ((0, 0), (0, GDN_CHUNK - dec_seq), (0, 0)))
        return a.reshape(dec_batch * GDN_CHUNK, a.shape[-1])

    conv0 = jnp.pad(state_conv[0].astype(F32), ((0, 0), (SUBLANES - (CONV_W - 1), 0), (0, 0)))
    og_s, s_s = _gdn(per_seq_chunk(c_s), per_seq_chunk(ab_s), cw, alog, dtb,
                     state_gdn[0].astype(F32), conv0, GDN_CHUNK, dec_seq, 1)
    og_s = og_s.reshape(dec_batch, GDN_CHUNK, GDN_VW)[:, :dec_seq].reshape(dec_batch * dec_seq, GDN_VW)
    n_phys, page = cache_k.shape[1], cache_k.shape[2]

    def pages_t(cache):
        return jnp.transpose(cache, (0, 2, 3, 1)).reshape(n_phys, SB_W, page)

    os_s = _sb_decode(q_s.reshape(dec_batch, dec_seq, SB_W), k_s.reshape(dec_batch, dec_seq, SB_W),
                      v_s.reshape(dec_batch, dec_seq, SB_W),
                      pages_t(cache_k[0]), pages_t(cache_v[0]), page_table,
                      jnp.broadcast_to(jnp.repeat(bias, dec_seq)[:, None], (SB_HEADS * dec_seq, BLK)),
                      tw, DECODE_PAGES_PER_STEP)
    y_s = _merge(xs, (og_s, zg_s, os_s.reshape(dec_batch * dec_seq, SB_W), zs_s), lig, lib, norm_g,
                 w_o, log_, lob, 0, alpha)

    def heads_last(t):
        return jnp.transpose(t.reshape(SB_HEADS, SB_HEAD_DIM, length), (2, 0, 1))[None, None]

    kv_s = (1, dec_batch, dec_seq, SB_HEADS, SB_HEAD_DIM)
    c_s3 = c_s.reshape(dec_batch, dec_seq, CONV_DIM)
    conv_s = jnp.concatenate([state_conv[0].astype(F32), c_s3], axis=1)[:, -(CONV_W - 1):]
    return (y_p[None], y_s.reshape(dec_batch, dec_seq, D_MODEL),
            heads_last(kt32_p), heads_last(vt32_p),
            k_s.reshape(kv_s), v_s.reshape(kv_s),
            s_p[None], s_s[None],
            c_p[length - (CONV_W - 1):length].reshape(1, 1, CONV_W - 1, CONV_DIM), conv_s[None])
```

```python
import functools
import math

import jax
import jax.numpy as jnp
from jax import lax
from jax.experimental import pallas as pl
from jax.experimental.pallas import tpu as pltpu

F32 = jnp.float32
BF16 = jnp.bfloat16

D_MODEL = 1024
N_META = 16
GDN_HEADS = 4
GDN_DK = 128
GDN_DV = 128
GDN_CHUNK = 64
CONV_W = 4
SB_HEADS = 8
SB_HEAD_DIM = 64
GDN_QK = GDN_HEADS * GDN_DK
GDN_VW = GDN_HEADS * GDN_DV
CONV_DIM = 2 * GDN_QK + GDN_VW
SB_W = SB_HEADS * SB_HEAD_DIM
PROJ_SIZES = (CONV_DIM, GDN_VW, GDN_HEADS, GDN_HEADS, SB_W, SB_W, SB_W, SB_W)
LN_EPS = 1e-5
RMS_EPS = 1e-6
L2_EPS = 1e-6
LOG2E = 1.4426950408889634

LANES = 128
SUBLANES = 8
BLK = 128
SB_BQ = 256
DECODE_PAGES_PER_STEP = 32
MERGE_ROWS = 256
PROJ_ROWS = 256
GDN_CHUNKS_PER_STEP = 4
VMEM_LIMIT = 52 * 1024 * 1024
COL_C = 0
COL_ZG = COL_C + CONV_DIM
COL_Q = COL_ZG + GDN_VW
COL_K = COL_Q + SB_W
COL_V = COL_K + SB_W
COL_ZS = COL_V + SB_W
COL_AB = COL_ZS + SB_W
PROJ_COLS = COL_AB + LANES


def _dot(a, b):
    return jnp.dot(a, b, preferred_element_type=F32)


def _dot_nt(a, b):
    return lax.dot_general(a, b, (((1,), (1,)), ((), ())), preferred_element_type=F32)


def _split2(x):
    hi = x.astype(BF16)
    return hi, (x - hi.astype(F32)).astype(BF16)


def _dot1(a, b):
    return _dot(a.astype(BF16), b.astype(BF16))


def _dot1_nt(a, b):
    return _dot_nt(a.astype(BF16), b.astype(BF16))


def _dot3(a, b):
    a_hi, a_lo = _split2(a)
    b_hi, b_lo = _split2(b)
    return _dot(a_hi, b_hi) + (_dot(a_hi, b_lo) + _dot(a_lo, b_hi))


def _layernorm(x, g, b):
    mu = jnp.mean(x, axis=-1, keepdims=True)
    xc = x - mu
    var = jnp.mean(xc * xc, axis=-1, keepdims=True)
    return xc * lax.rsqrt(var + LN_EPS) * g + b


def _sigmoid(x):
    return 1.0 / (1.0 + jnp.exp(-x))


def _softplus(x):
    return jnp.maximum(x, 0.0) + jnp.log(1.0 + jnp.exp(-jnp.abs(x)))


def _project_kernel(n_head, q_scale, *refs):
    if n_head:
        x_ref, head_ref, g_ref, b_ref, w_ref = refs[:5]
        (c_ref, zg_ref, ab_ref, q_ref, zs_ref, kt32_ref, vt32_ref, qt_ref, kb_ref, vt_ref,
         tail_ref, kv_ref) = refs[5:]

        @pl.when(pl.program_id(0) == 0)
        def _():
            tail_ref[...] = head_ref[...]

        xb = x_ref[...]
        rows = xb.shape[0]
        x = jnp.concatenate([tail_ref[...], xb[:rows - n_head]], axis=0)
        tail_ref[...] = xb[rows - n_head:]
    else:
        x_ref, g_ref, b_ref, w_ref = refs[:4]
        c_ref, zg_ref, ab_ref, q_ref, zs_ref, k_ref, v_ref = refs[4:]
        x = x_ref[...]
    h = _layernorm(x, g_ref[...], b_ref[...]).astype(BF16)

    def proj(col, width):
        return _dot(h, w_ref[:, col:col + width])

    c_ref[...] = proj(COL_C, CONV_DIM)
    zg_ref[...] = proj(COL_ZG, GDN_VW)
    ab_ref[...] = proj(COL_AB, LANES)
    q = proj(COL_Q, SB_W) * q_scale
    q_ref[...] = q
    zs_ref[...] = proj(COL_ZS, SB_W)
    k = proj(COL_K, SB_W)
    v = proj(COL_V, SB_W)
    if n_head:
        kv_ref[0] = k
        kv_ref[1] = v
        kv_ref[2] = q
        kt32_ref[...] = kv_ref[0].T
        vt = kv_ref[1].T
        vt32_ref[...] = vt
        qt = kv_ref[2].T.astype(BF16)
        kb = k.astype(BF16)
        vt = vt.astype(BF16)
        for s in range(qt_ref.shape[0]):
            qt_ref[s] = qt[:, s * BLK:(s + 1) * BLK]
            kb_ref[s] = kb[s * BLK:(s + 1) * BLK]
            vt_ref[s] = vt[:, s * BLK:(s + 1) * BLK]
    else:
        k_ref[...] = k
        v_ref[...] = v


def _project(x, head, ln_g, ln_b, w, q_scale, n_blk):
    n_head = 0 if head is None else head.shape[0]
    rows = PROJ_ROWS
    sub = rows // BLK
    n_x_blk = x.shape[0] // rows
    assert x.shape[0] % rows == 0
    n = n_blk * rows
    length = x.shape[0] + n_head
    last_t = (length - 1) // rows

    def row(width):
        return pl.BlockSpec((rows, width), lambda i: (i, 0))

    def const(shape):
        return pl.BlockSpec(shape, lambda i: (0,) * len(shape))

    in_specs = [pl.BlockSpec((rows, D_MODEL), lambda i: (jnp.minimum(i, n_x_blk - 1), 0))]
    args = [x]
    if n_head:
        in_specs.append(const((n_head, D_MODEL)))
        args.append(head)
    in_specs += [const((1, D_MODEL)), const((1, D_MODEL)), const((D_MODEL, PROJ_COLS))]
    args += [ln_g, ln_b, w]
    out_shape = [jax.ShapeDtypeStruct((n, width), F32) for width in (CONV_DIM, GDN_VW, LANES, SB_W, SB_W)]
    out_specs = [row(width) for width in (CONV_DIM, GDN_VW, LANES, SB_W, SB_W)]
    scratch = []
    if n_head:
        out_shape += [jax.ShapeDtypeStruct((SB_W, length), F32),
                      jax.ShapeDtypeStruct((SB_W, length), F32),
                      jax.ShapeDtypeStruct((n_blk * sub, SB_W, BLK), BF16),
                      jax.ShapeDtypeStruct((n_blk * sub, BLK, SB_W), BF16),
                      jax.ShapeDtypeStruct((n_blk * sub, SB_W, BLK), BF16)]
        out_specs += [pl.BlockSpec((SB_W, rows), lambda i: (0, jnp.minimum(i, last_t))),
                      pl.BlockSpec((SB_W, rows), lambda i: (0, jnp.minimum(i, last_t))),
                      pl.BlockSpec((sub, SB_W, BLK), lambda i: (i, 0, 0)),
                      pl.BlockSpec((sub, BLK, SB_W), lambda i: (i, 0, 0)),
                      pl.BlockSpec((sub, SB_W, BLK), lambda i: (i, 0, 0))]
        scratch = [pltpu.VMEM((n_head, D_MODEL), F32), pltpu.VMEM((3, rows, SB_W), F32)]
    else:
        out_shape += [jax.ShapeDtypeStruct((n, SB_W), F32)] * 2
        out_specs += [row(SB_W), row(SB_W)]
    return pl.pallas_call(
        functools.partial(_project_kernel, n_head, q_scale),
        grid=(n_blk,),
        in_specs=in_specs,
        out_specs=out_specs,
        out_shape=out_shape,
        scratch_shapes=scratch,
        compiler_params=pltpu.CompilerParams(
            dimension_semantics=("arbitrary",), vmem_limit_bytes=VMEM_LIMIT),
        name="project",
    )(*args)


def _gdn_kernel(chunk, n_sub, hi, c_ref, ab_ref, cw_ref, alog_ref, dt_ref, s0_ref, conv0_ref,
                tri_ref, eye_ref, o_ref, s_ref, xs_ref):
    j = pl.program_id(1)
    C = chunk
    R = C * n_sub

    @pl.when(j == 0)
    def _():
        xs_ref[0:SUBLANES, :] = conv0_ref[0]
        s_ref[...] = s0_ref[...]

    valid = j * R + lax.broadcasted_iota(jnp.int32, (R, 1), 0) < hi

    xs_ref[SUBLANES:SUBLANES + R, :] = jnp.where(valid, c_ref[...], 0.0)
    first = SUBLANES - (CONV_W - 1)
    y = xs_ref[first:first + R, :] * cw_ref[0:1, :]
    for i in range(1, CONV_W):
        y = y + xs_ref[first + i:first + i + R, :] * cw_ref[i:i + 1, :]
    xs_ref[0:SUBLANES, :] = xs_ref[R:R + SUBLANES, :]
    y = y * _sigmoid(y)

    ab = ab_ref[...]
    g_all = jnp.where(valid, -jnp.exp(alog_ref[...]) * _softplus(ab + dt_ref[...]), 0.0)
    beta_all = jnp.where(valid, _sigmoid(ab), 0.0)
    ri = lax.broadcasted_iota(jnp.int32, (C, C), 0)
    ci = lax.broadcasted_iota(jnp.int32, (C, C), 1)
    causal = ri >= ci
    strict = ri > ci
    tri = tri_ref[...]
    eye = eye_ref[...]

    prep = []
    for cc in range(n_sub):
        rows = slice(cc * C, (cc + 1) * C)
        g = g_all[rows]
        beta = beta_all[rows]
        g_hi = g.astype(BF16)
        g_rest = g - g_hi.astype(F32)
        g_mid, g_lo = _split2(g_rest)
        gc = _dot(tri, g_hi) + (_dot(tri, g_mid) + _dot(tri, g_lo))
        gct = gc.T
        for h in range(GDN_HEADS):
            q = y[rows, h * GDN_DK:(h + 1) * GDN_DK]
            k = y[rows, GDN_QK + h * GDN_DK:GDN_QK + (h + 1) * GDN_DK]
            v = y[rows, 2 * GDN_QK + h * GDN_DV:2 * GDN_QK + (h + 1) * GDN_DV]
            q = q * lax.rsqrt(jnp.sum(q * q, axis=-1, keepdims=True) + L2_EPS) * (GDN_DK ** -0.5)
            k = k * lax.rsqrt(jnp.sum(k * k, axis=-1, keepdims=True) + L2_EPS)
            gcol = gc[:, h:h + 1]
            grow = gct[h:h + 1, :]
            bcol = beta[:, GDN_HEADS + h:GDN_HEADS + h + 1]
            glast = gc[C - 1:C, h:h + 1]
            decay = jnp.exp(jnp.where(causal, gcol - grow, -jnp.inf))
            kb = k * bcol
            eg = jnp.exp(gcol)
            prep.append(dict(
                m=jnp.where(strict, _dot1_nt(kb, k) * decay, 0.0),
                rhs=jnp.concatenate([v * bcol, kb * eg], axis=-1),
                qk=_dot1_nt(q, k) * decay, qd=q * eg,
                kdt=(k * jnp.exp(glast - gcol)).T,
                eg_last=jnp.exp(glast)))

    def dot3_split(a_split, b_split):
        (a_hi, a_lo), (b_hi, b_lo) = a_split, b_split
        return _dot(a_hi, b_hi) + (_dot(a_hi, b_lo) + _dot(a_lo, b_hi))

    ps = [-pr["m"] for pr in prep]
    ts = [eye + p for p in ps]
    p_splits = [_split2(p) for p in ps]
    for _ in range(int(math.log2(C)) - 1):
        p_splits = [_split2(dot3_split(sp, sp)) for sp in p_splits]
        ts = [t + dot3_split(_split2(t), sp) for t, sp in zip(ts, p_splits)]
    sols = [_dot3(t, pr["rhs"]) for t, pr in zip(ts, prep)]

    for cc in range(n_sub):
        ids = [cc * GDN_HEADS + h for h in range(GDN_HEADS)]
        s = [s_ref[0, h] for h in range(GDN_HEADS)]
        v_new = [sols[i][:, :GDN_DV] - _dot1(sols[i][:, GDN_DV:], s[h]) for h, i in enumerate(ids)]
        for h, i in enumerate(ids):
            s_ref[0, h] = s[h] * prep[i]["eg_last"] + _dot1(prep[i]["kdt"], v_new[h])
        for h, i in enumerate(ids):
            o_ref[cc * C:(cc + 1) * C, h * GDN_DV:(h + 1) * GDN_DV] = (
                _dot1(prep[i]["qd"], s[h]) + _dot1(prep[i]["qk"], v_new[h]))


def _gdn(c, ab, conv_w, a_log, dt_bias, s0, conv0, rows_per_seq, hi, n_sub):
    b = s0.shape[0]
    rows = GDN_CHUNK * n_sub
    n_steps = rows_per_seq // rows
    return pl.pallas_call(
        functools.partial(_gdn_kernel, GDN_CHUNK, n_sub, hi),
        grid=(b, n_steps),
        in_specs=[
            pl.BlockSpec((rows, CONV_DIM), lambda bi, j: (bi * n_steps + j, 0)),
            pl.BlockSpec((rows, LANES), lambda bi, j: (bi * n_steps + j, 0)),
            pl.BlockSpec((SUBLANES, CONV_DIM), lambda bi, j: (0, 0)),
            pl.BlockSpec((1, LANES), lambda bi, j: (0, 0)),
            pl.BlockSpec((1, LANES), lambda bi, j: (0, 0)),
            pl.BlockSpec((1, GDN_HEADS, GDN_DK, GDN_DV), lambda bi, j: (bi, 0, 0, 0)),
            pl.BlockSpec((1, SUBLANES, CONV_DIM), lambda bi, j: (bi, 0, 0)),
            pl.BlockSpec((GDN_CHUNK, GDN_CHUNK), lambda bi, j: (0, 0)),
            pl.BlockSpec((GDN_CHUNK, GDN_CHUNK), lambda bi, j: (0, 0)),
        ],
        out_specs=[
            pl.BlockSpec((rows, GDN_VW), lambda bi, j: (bi * n_steps + j, 0)),
            pl.BlockSpec((1, GDN_HEADS, GDN_DK, GDN_DV), lambda bi, j: (bi, 0, 0, 0)),
        ],
        out_shape=[
            jax.ShapeDtypeStruct((b * rows_per_seq, GDN_VW), F32),
            jax.ShapeDtypeStruct((b, GDN_HEADS, GDN_DK, GDN_DV), F32),
        ],
        scratch_shapes=[pltpu.VMEM((rows + SUBLANES, CONV_DIM), F32)],
        compiler_params=pltpu.CompilerParams(
            dimension_semantics=("arbitrary", "arbitrary"), vmem_limit_bytes=VMEM_LIMIT),
        name="gdn",
    )(c, ab, conv_w, a_log, dt_bias, s0, conv0,
      jnp.tril(jnp.ones((GDN_CHUNK, GDN_CHUNK), BF16)), jnp.eye(GDN_CHUNK, dtype=F32))


def _cumsum_weights():
    r = jnp.arange(BLK)[:, None]
    c = jnp.arange(2 * BLK)[None, :]
    return jnp.where(c < BLK, r >= c, True).astype(BF16)


def _softplus2(t2):
    neg_abs = lax.bitcast_convert_type(
        lax.bitcast_convert_type(t2, jnp.uint32) | jnp.uint32(0x80000000), F32)
    return jnp.maximum(t2, 0.0) + jnp.log(1.0 + jnp.exp2(neg_abs)) * LOG2E


def _block_sums(sp, tw):
    res = _dot(sp.astype(BF16), tw)
    return res[:, :BLK], res[:, BLK:]


def _sb_tile(t2, vis, carry, tw):
    sp = _softplus2(t2)
    if vis is not None:
        sp = jnp.where(vis, sp, 0.0)
    incl, total = _block_sums(sp, tw)
    a = jnp.exp2(t2 - incl - carry)
    if vis is not None:
        a = jnp.where(vis, a, 0.0)
    return a, carry + total


def _sb_prompt_kernel(bias_ref, qt_ref, kb_ref, vt_ref, u_ref, o_ref,
                      wq_ref, carry_ref, acc_ref, t2_ref, a_ref):
    i = pl.program_id(0)
    sub = qt_ref.shape[0]
    bq = sub * BLK
    pairs = range(SB_HEADS // 2)
    u = u_ref[...]
    kidx = lax.broadcasted_iota(jnp.int32, (BLK, bq), 0)
    qidx = i * bq + lax.broadcasted_iota(jnp.int32, (BLK, bq), 1)
    zeros_q = jnp.zeros((SB_HEAD_DIM, bq), BF16)
    zeros_k = jnp.zeros((SB_HEAD_DIM, BLK), BF16)

    def pair_cols(p):
        return slice(2 * SB_HEAD_DIM * p, 2 * SB_HEAD_DIM * (p + 1))

    for p in pairs:
        qtp = jnp.concatenate([qt_ref[s, pair_cols(p), :] for s in range(sub)], axis=1)
        wq_ref[p] = jnp.concatenate([
            jnp.concatenate([qtp[:SB_HEAD_DIM], zeros_q], axis=1),
            jnp.concatenate([zeros_q, qtp[SB_HEAD_DIM:]], axis=1)], axis=0)
    carry_ref[...] = jnp.zeros_like(carry_ref)
    acc_ref[...] = jnp.zeros_like(acc_ref)

    def logits(j, p):
        return _dot(kb_ref[j, :, pair_cols(p)], wq_ref[p])

    def block_sums(lp, p, vis):
        out = []
        for hh in range(2):
            t2 = lp[:, hh * bq:(hh + 1) * bq] + bias_ref[2 * p + hh]
            sp = _softplus2(t2)
            if vis is not None:
                sp = jnp.where(vis, sp, 0.0)
            res = _dot(u, sp.astype(BF16))
            out.append((t2, res[:BLK], res[BLK:BLK + SUBLANES]))
        return out

    def weights(p, staged, vis):
        a = []
        for hh, (t2, incl, total) in enumerate(staged):
            h = 2 * p + hh
            carry = carry_ref[h]
            e = (t2 - incl).reshape(BLK // SUBLANES, SUBLANES, bq) - carry[None]
            a_h = jnp.exp2(e).reshape(BLK, bq)
            if vis is not None:
                a_h = jnp.where(vis, a_h, 0.0)
            carry_ref[h] = carry + total
            a.append(a_h.astype(BF16))
        return jnp.concatenate(a, axis=0)

    def values(j, p, ap):
        vtp = vt_ref[j, pair_cols(p), :]
        vbdt = jnp.concatenate([
            jnp.concatenate([vtp[:SB_HEAD_DIM], zeros_k], axis=1),
            jnp.concatenate([zeros_k, vtp[SB_HEAD_DIM:]], axis=1)], axis=0)
        acc_ref[p] += _dot(vbdt, ap)

    for d in range(sub - 1, -1, -1):
        jd = i * sub + d
        vis = jd * BLK + kidx < qidx
        staged = [block_sums(logits(jd, p), p, vis) for p in pairs]
        for p in pairs:
            values(jd, p, weights(p, staged[p], vis))

    n = i * sub
    for p in pairs:
        t2_ref[p] = logits(jnp.maximum(n - 1, 0), p)
    a_ref[...] = jnp.zeros_like(a_ref)

    def trip(j):
        j_next = jnp.maximum(j - 1, 0)
        staged = []
        for p in pairs:
            lp = t2_ref[p]
            t2_ref[p] = logits(j_next, p)
            staged.append(block_sums(lp, p, None))
            values(j + 1, p, a_ref[p])
        for p in pairs:
            a_ref[p] = weights(p, staged[p], None)

    def body(jj, _):
        for d in range(sub):
            trip(n - 1 - (jj * sub + d))
        return 0

    lax.fori_loop(0, i, body, 0)
    for p in pairs:
        values(0, p, a_ref[p])
        o_ref[:, pair_cols(p)] = acc_ref[p].T


def _sb_prompt(qt, kb, vt, bias2, u, bq):
    n_kblk = kb.shape[0]
    sub = bq // BLK
    assert n_kblk % sub == 0
    return pl.pallas_call(
        _sb_prompt_kernel,
        grid=(n_kblk // sub,),
        in_specs=[
            pl.BlockSpec(memory_space=pltpu.SMEM),
            pl.BlockSpec((sub, SB_W, BLK), lambda i: (i, 0, 0)),
            pl.BlockSpec((n_kblk, BLK, SB_W), lambda i: (0, 0, 0), pipeline_mode=pl.Buffered(1)),
            pl.BlockSpec((n_kblk, SB_W, BLK), lambda i: (0, 0, 0), pipeline_mode=pl.Buffered(1)),
            pl.BlockSpec((BLK + 2 * SUBLANES, BLK), lambda i: (0, 0)),
        ],
        out_specs=pl.BlockSpec((bq, SB_W), lambda i: (i, 0)),
        out_shape=jax.ShapeDtypeStruct((n_kblk * BLK, SB_W), F32),
        scratch_shapes=[pltpu.VMEM((SB_HEADS // 2, 2 * SB_HEAD_DIM, 2 * bq), BF16),
                        pltpu.VMEM((SB_HEADS, SUBLANES, bq), F32),
                        pltpu.VMEM((SB_HEADS // 2, 2 * SB_HEAD_DIM, bq), F32),
                        pltpu.VMEM((SB_HEADS // 2, BLK, 2 * bq), F32),
                        pltpu.VMEM((SB_HEADS // 2, 2 * BLK, bq), BF16)],
        compiler_params=pltpu.CompilerParams(
            dimension_semantics=("arbitrary",), vmem_limit_bytes=VMEM_LIMIT),
        name="sb_prompt",
    )(bias2, qt, kb, vt, u)


def _sb_decode_kernel(t_new, pages_per_step, pt_ref, q_ref, kn_ref, vn_ref, bias_ref, tw_ref, *refs):
    k_refs = refs[:pages_per_step]
    v_refs = refs[pages_per_step:2 * pages_per_step]
    o_ref, qbd_ref, carry_ref, acc_ref = refs[2 * pages_per_step:]
    j = pl.program_id(1)
    rows = SB_HEADS * t_new
    tw = tw_ref[...]
    bias = bias_ref[...]
    head_of_lane = lax.broadcasted_iota(jnp.int32, (t_new, SB_W), 1) // SB_HEAD_DIM

    @pl.when(j == 0)
    def _():
        q = q_ref[0]
        qbd_ref[...] = jnp.concatenate(
            [jnp.where(head_of_lane == h, q, 0.0) for h in range(SB_HEADS)], axis=0).astype(BF16)
        pad_rows = jnp.zeros((BLK - t_new, SB_W), F32)
        knt = jnp.concatenate([kn_ref[0], pad_rows], axis=0).T.astype(BF16)
        vnt = jnp.concatenate([vn_ref[0], pad_rows], axis=0).T.astype(BF16)
        tq = lax.broadcasted_iota(jnp.int32, (rows, BLK), 0) % t_new
        ks = lax.broadcasted_iota(jnp.int32, (rows, BLK), 1)
        t2 = _dot(qbd_ref[...], knt) + bias
        a, c = _sb_tile(t2, ks < tq, jnp.zeros((rows, BLK), F32), tw)
        carry_ref[...] = c
        acc_ref[...] = _dot_nt(a.astype(BF16), vnt)

    qbd = qbd_ref[...]
    t2s = [_dot(qbd, k_refs[r][0].astype(BF16)) + bias for r in range(pages_per_step)]
    sps = [_softplus2(t2) for t2 in t2s]
    incl, total = _block_sums(jnp.concatenate(sps, axis=0), tw)
    carry = carry_ref[...]
    acc = acc_ref[...]
    for r in range(pages_per_step):
        rs = slice(r * rows, (r + 1) * rows)
        a = jnp.exp2(t2s[r] - incl[rs] - carry)
        carry = carry + total[rs]
        acc = acc + _dot_nt(a.astype(BF16), v_refs[r][0].astype(BF16))
    carry_ref[...] = carry
    acc_ref[...] = acc

    @pl.when(j == pl.num_programs(1) - 1)
    def _():
        out = jnp.zeros((t_new, SB_W), F32)
        for h in range(SB_HEADS):
            out = out + jnp.where(head_of_lane == h, acc[h * t_new:(h + 1) * t_new], 0.0)
        o_ref[0] = out


def _sb_decode(q, k_new, v_new, cache_kt, cache_vt, page_table, bias_rows, tw, pages_per_step):
    b, t_new, _ = q.shape
    n_pages = page_table.shape[1]
    page = cache_kt.shape[2]
    assert page == BLK and n_pages % pages_per_step == 0
    n_steps = n_pages // pages_per_step
    rows = SB_HEADS * t_new

    def page_spec(r):
        return pl.BlockSpec(
            (1, SB_W, page),
            lambda bi, j, pt: (pt[bi, n_pages - 1 - (j * pages_per_step + r)], 0, 0))

    seq = pl.BlockSpec((1, t_new, SB_W), lambda bi, j, pt: (bi, 0, 0))
    grid_spec = pltpu.PrefetchScalarGridSpec(
        num_scalar_prefetch=1,
        grid=(b, n_steps),
        in_specs=[seq, seq, seq,
                  pl.BlockSpec((rows, BLK), lambda bi, j, pt: (0, 0)),
                  pl.BlockSpec((BLK, 2 * BLK), lambda bi, j, pt: (0, 0))]
        + [page_spec(r) for r in range(pages_per_step)]
        + [page_spec(r) for r in range(pages_per_step)],
        out_specs=seq,
        scratch_shapes=[pltpu.VMEM((rows, SB_W), BF16),
                        pltpu.VMEM((rows, BLK), F32),
                        pltpu.VMEM((rows, SB_W), F32)],
    )
    return pl.pallas_call(
        functools.partial(_sb_decode_kernel, t_new, pages_per_step),
        grid_spec=grid_spec,
        out_shape=jax.ShapeDtypeStruct((b, t_new, SB_W), F32),
        compiler_params=pltpu.CompilerParams(
            dimension_semantics=("arbitrary", "arbitrary"), vmem_limit_bytes=VMEM_LIMIT),
        name="sb_decode",
    )(page_table, q, k_new, v_new, bias_rows, tw,
      *([cache_kt] * pages_per_step), *([cache_vt] * pages_per_step))


def _merge_kernel(alpha, shift, *refs):
    x_ref = refs[0]
    n_mix = 4
    if shift:
        mixer = [jnp.concatenate([refs[1 + 2 * m][shift:, :], refs[2 + 2 * m][...]], axis=0)
                 for m in range(n_mix)]
        refs = refs[1 + 2 * n_mix:]
    else:
        mixer = [refs[1 + m][...] for m in range(n_mix)]
        refs = refs[1 + n_mix:]
    og, zg, osb, zs = mixer
    lig_ref, lib_ref, ng_ref, w_ref, log_ref, lob_ref, y_ref = refs
    h = _layernorm(x_ref[...], lig_ref[...], lib_ref[...])
    parts = []
    for hd in range(GDN_HEADS):
        sl = slice(hd * GDN_DV, (hd + 1) * GDN_DV)
        o = og[:, sl]
        o = o * lax.rsqrt(jnp.mean(o * o, axis=-1, keepdims=True) + RMS_EPS) * ng_ref[...]
        z = zg[:, sl]
        parts.append(o * (z * _sigmoid(z)))
    parts.append(osb * (zs * _sigmoid(zs)))
    cat = jnp.concatenate(parts, axis=-1).astype(BF16)
    mix = _dot(cat, w_ref[...])
    y_ref[...] = _layernorm(alpha * h + mix, log_ref[...], lob_ref[...])


def _merge(x, mixer, ln_in_g, ln_in_b, norm_g, w_out, ln_out_g, ln_out_b, shift, alpha):
    n = x.shape[0]
    rows = MERGE_ROWS
    n_blk = n // rows
    assert n % rows == 0 and (rows % shift == 0 if shift else True)
    vec = pl.BlockSpec((1, D_MODEL), lambda i: (0, 0))
    in_specs = [pl.BlockSpec((rows, D_MODEL), lambda i: (i, 0))]
    args = [x]
    for m in mixer:
        in_specs.append(pl.BlockSpec((rows, m.shape[1]), lambda i: (i, 0)))
        args.append(m)
        if shift:
            per_blk = rows // shift
            in_specs.append(pl.BlockSpec((shift, m.shape[1]), lambda i: ((i + 1) * per_blk, 0)))
            args.append(m)
    in_specs += [vec, vec, pl.BlockSpec((1, GDN_DV), lambda i: (0, 0)),
                 pl.BlockSpec((D_MODEL, D_MODEL), lambda i: (0, 0)), vec, vec]
    args += [ln_in_g, ln_in_b, norm_g, w_out, ln_out_g, ln_out_b]
    return pl.pallas_call(
        functools.partial(_merge_kernel, alpha, shift),
        grid=(n_blk,),
        in_specs=in_specs,
        out_specs=pl.BlockSpec((rows, D_MODEL), lambda i: (i, 0)),
        out_shape=jax.ShapeDtypeStruct((n, D_MODEL), F32),
        compiler_params=pltpu.CompilerParams(
            dimension_semantics=("arbitrary",), vmem_limit_bytes=VMEM_LIMIT),
        name="merge",
    )(*args)


def _reorder_w_in(w_in):
    offs = [0]
    for s in PROJ_SIZES:
        offs.append(offs[-1] + s)
    c, zg, a, b, q, k, v, zs = [w_in[:, offs[i]:offs[i + 1]] for i in range(len(PROJ_SIZES))]
    pad = jnp.zeros((w_in.shape[0], LANES - 2 * GDN_HEADS), w_in.dtype)
    return jnp.concatenate([c, zg, q, k, v, zs, a, b, pad], axis=1).astype(BF16)


def _lane_row(v, width=LANES):
    return jnp.pad(v.astype(F32), (0, width - v.shape[0])).reshape(1, width)


def kernel(x_prompt, x_sample, cache_k, cache_v, page_table, state_gdn, state_conv, meta_tokens,
           ln_in_g, ln_in_b, w_in, conv_w, a_log, dt_bias, gdn_norm_g, sb_bias, w_out, ln_out_g,
           ln_out_b):
    depth = w_in.shape[0]
    assert depth == 1, "single-layer trunk only"
    batch, seq, _ = x_prompt.shape
    dec_batch, dec_seq, _ = x_sample.shape
    assert batch == 1 and seq % BLK == 0 and meta_tokens.shape[0] == N_META
    assert (dec_batch * dec_seq) % BLK == 0 and dec_seq <= GDN_CHUNK
    alpha = (2 * depth) ** 0.25
    length = N_META + seq
    n_pad = -(-length // SB_BQ) * SB_BQ
    q_scale = LOG2E * SB_HEAD_DIM ** -0.5

    lig = ln_in_g.reshape(1, D_MODEL)
    lib = ln_in_b.reshape(1, D_MODEL)
    w = _reorder_w_in(w_in[0])
    w_o = w_out[0].astype(BF16)
    cw = jnp.pad(conv_w[0], ((0, SUBLANES - CONV_W), (0, 0)))
    alog = _lane_row(a_log[0])
    dtb = _lane_row(dt_bias[0])
    norm_g = gdn_norm_g[0].reshape(1, GDN_DV)
    log_ = ln_out_g[0].reshape(1, D_MODEL)
    lob = ln_out_b[0].reshape(1, D_MODEL)
    tw = _cumsum_weights()
    bias = sb_bias[0].astype(F32) * LOG2E

    xp = x_prompt[0]
    c_p, zg_p, ab_p, _, zs_p, kt32_p, vt32_p, qt_p, kb_p, vt_p = _project(
        xp, meta_tokens.astype(F32), lig, lib, w, q_scale, n_pad // PROJ_ROWS)
    og_p, s_p = _gdn(c_p, ab_p, cw, alog, dtb,
                     jnp.zeros((1, GDN_HEADS, GDN_DK, GDN_DV), F32),
                     jnp.zeros((1, SUBLANES, CONV_DIM), F32), n_pad, length, GDN_CHUNKS_PER_STEP)
    idx = jnp.arange(BLK)
    u = jnp.concatenate([idx[None, :] >= idx[:, None], jnp.ones((2 * SUBLANES, BLK), bool)]).astype(BF16)
    os_p = _sb_prompt(qt_p, kb_p, vt_p, bias, u, SB_BQ)
    y_p = _merge(xp, (og_p, zg_p, os_p, zs_p), lig, lib, norm_g, w_o, log_, lob, N_META, alpha)

    xs = x_sample.reshape(dec_batch * dec_seq, D_MODEL)
    c_s, zg_s, ab_s, q_s, zs_s, k_s, v_s = _project(xs, None, lig, lib, w, q_scale,
                                                     dec_batch * dec_seq // PROJ_ROWS)

    def per_seq_chunk(a):
        a = a.reshape(dec_batch, dec_seq, a.shape[-1])
        a = jnp.pad(a, ((0, 0), (0, GDN_CHUNK - dec_seq), (0, 0)))
        return a.reshape(dec_batch * GDN_CHUNK, a.shape[-1])

    conv0 = jnp.pad(state_conv[0].astype(F32), ((0, 0), (SUBLANES - (CONV_W - 1), 0), (0, 0)))
    og_s, s_s = _gdn(per_seq_chunk(c_s), per_seq_chunk(ab_s), cw, alog, dtb,
                     state_gdn[0].astype(F32), conv0, GDN_CHUNK, dec_seq, 1)
    og_s = og_s.reshape(dec_batch, GDN_CHUNK, GDN_VW)[:, :dec_seq].reshape(dec_batch * dec_seq, GDN_VW)
    n_phys, page = cache_k.shape[1], cache_k.shape[2]

    def pages_t(cache):
        return jnp.transpose(cache, (0, 2, 3, 1)).reshape(n_phys, SB_W, page)

    os_s = _sb_decode(q_s.reshape(dec_batch, dec_seq, SB_W), k_s.reshape(dec_batch, dec_seq, SB_W),
                      v_s.reshape(dec_batch, dec_seq, SB_W),
                      pages_t(cache_k[0]), pages_t(cache_v[0]), page_table,
                      jnp.broadcast_to(jnp.repeat(bias, dec_seq)[:, None], (SB_HEADS * dec_seq, BLK)),
                      tw, DECODE_PAGES_PER_STEP)
    y_s = _merge(xs, (og_s, zg_s, os_s.reshape(dec_batch * dec_seq, SB_W), zs_s), lig, lib, norm_g,
                 w_o, log_, lob, 0, alpha)

    def heads_last(t):
        return jnp.transpose(t.reshape(SB_HEADS, SB_HEAD_DIM, length), (2, 0, 1))[None, None]

    kv_s = (1, dec_batch, dec_seq, SB_HEADS, SB_HEAD_DIM)
    c_s3 = c_s.reshape(dec_batch, dec_seq, CONV_DIM)
    conv_s = jnp.concatenate([state_conv[0].astype(F32), c_s3], axis=1)[:, -(CONV_W - 1):]
    return (y_p[None], y_s.reshape(dec_batch, dec_seq, D_MODEL),
            heads_last(kt32_p), heads_last(vt32_p),
            k_s.reshape(kv_s), v_s.reshape(kv_s),
            s_p[None], s_s[None],
            c_p[length - (CONV_W - 1):length].reshape(1, 1, CONV_W - 1, CONV_DIM), conv_s[None])
```

```python
import functools
import math

import jax
import jax.numpy as jnp
from jax import lax
from jax.experimental import pallas as pl
from jax.experimental.pallas import tpu as pltpu

F32 = jnp.float32
BF16 = jnp.bfloat16

D_MODEL = 1024
N_META = 16
GDN_HEADS = 4
GDN_DK = 128
GDN_DV = 128
GDN_CHUNK = 64
CONV_W = 4
SB_HEADS = 8
SB_HEAD_DIM = 64
GDN_QK = GDN_HEADS * GDN_DK
GDN_VW = GDN_HEADS * GDN_DV
CONV_DIM = 2 * GDN_QK + GDN_VW
SB_W = SB_HEADS * SB_HEAD_DIM
PROJ_SIZES = (CONV_DIM, GDN_VW, GDN_HEADS, GDN_HEADS, SB_W, SB_W, SB_W, SB_W)
LN_EPS = 1e-5
RMS_EPS = 1e-6
L2_EPS = 1e-6
LOG2E = 1.4426950408889634

LANES = 128
SUBLANES = 8
BLK = 128
SB_BQ = 256
DECODE_PAGES_PER_STEP = 32
MERGE_ROWS = 256
PROJ_ROWS = 256
GDN_CHUNKS_PER_STEP = 4
VMEM_LIMIT = 52 * 1024 * 1024
COL_C = 0
COL_ZG = COL_C + CONV_DIM
COL_Q = COL_ZG + GDN_VW
COL_K = COL_Q + SB_W
COL_V = COL_K + SB_W
COL_ZS = COL_V + SB_W
COL_AB = COL_ZS + SB_W
PROJ_COLS = COL_AB + LANES


def _dot(a, b):
    return jnp.dot(a, b, preferred_element_type=F32)


def _dot_nt(a, b):
    return lax.dot_general(a, b, (((1,), (1,)), ((), ())), preferred_element_type=F32)


def _split2(x):
    hi = x.astype(BF16)
    return hi, (x - hi.astype(F32)).astype(BF16)


def _dot1(a, b):
    return _dot(a.astype(BF16), b.astype(BF16))


def _dot1_nt(a, b):
    return _dot_nt(a.astype(BF16), b.astype(BF16))


def _dot3(a, b):
    a_hi, a_lo = _split2(a)
    b_hi, b_lo = _split2(b)
    return _dot(a_hi, b_hi) + (_dot(a_hi, b_lo) + _dot(a_lo, b_hi))


def _layernorm(x, g, b):
    mu = jnp.mean(x, axis=-1, keepdims=True)
    xc = x - mu
    var = jnp.mean(xc * xc, axis=-1, keepdims=True)
    return xc * lax.rsqrt(var + LN_EPS) * g + b


def _sigmoid(x):
    return 1.0 / (1.0 + jnp.exp(-x))


def _softplus(x):
    return jnp.maximum(x, 0.0) + jnp.log(1.0 + jnp.exp(-jnp.abs(x)))


def _project_kernel(n_head, q_scale, *refs):
    if n_head:
        x_ref, head_ref, g_ref, b_ref, w_ref = refs[:5]
        (c_ref, zg_ref, ab_ref, q_ref, zs_ref, kt32_ref, vt32_ref, qt_ref, kb_ref, vt_ref,
         tail_ref, kv_ref) = refs[5:]

        @pl.when(pl.program_id(0) == 0)
        def _():
            tail_ref[...] = head_ref[...]

        xb = x_ref[...]
        rows = xb.shape[0]
        x = jnp.concatenate([tail_ref[...], xb[:rows - n_head]], axis=0)
        tail_ref[...] = xb[rows - n_head:]
    else:
        x_ref, g_ref, b_ref, w_ref = refs[:4]
        c_ref, zg_ref, ab_ref, q_ref, zs_ref, k_ref, v_ref = refs[4:]
        x = x_ref[...]
    h = _layernorm(x, g_ref[...], b_ref[...]).astype(BF16)

    def proj(col, width):
        return _dot(h, w_ref[:, col:col + width])

    c_ref[...] = proj(COL_C, CONV_DIM)
    zg_ref[...] = proj(COL_ZG, GDN_VW)
    ab_ref[...] = proj(COL_AB, LANES)
    q = proj(COL_Q, SB_W) * q_scale
    q_ref[...] = q
    zs_ref[...] = proj(COL_ZS, SB_W)
    k = proj(COL_K, SB_W)
    v = proj(COL_V, SB_W)
    if n_head:
        kv_ref[0] = k
        kv_ref[1] = v
        kv_ref[2] = q
        kt32_ref[...] = kv_ref[0].T
        vt = kv_ref[1].T
        vt32_ref[...] = vt
        qt = kv_ref[2].T.astype(BF16)
        kb = k.astype(BF16)
        vt = vt.astype(BF16)
        for s in range(qt_ref.shape[0]):
            qt_ref[s] = qt[:, s * BLK:(s + 1) * BLK]
            kb_ref[s] = kb[s * BLK:(s + 1) * BLK]
            vt_ref[s] = vt[:, s * BLK:(s + 1) * BLK]
    else:
        k_ref[...] = k
        v_ref[...] = v


def _project(x, head, ln_g, ln_b, w, q_scale, n_blk):
    n_head = 0 if head is None else head.shape[0]
    rows = PROJ_ROWS
    sub = rows // BLK
    n_x_blk = x.shape[0] // rows
    assert x.shape[0] % rows == 0
    n = n_blk * rows
    length = x.shape[0] + n_head
    last_t = (length - 1) // rows

    def row(width):
        return pl.BlockSpec((rows, width), lambda i: (i, 0))

    def const(shape):
        return pl.BlockSpec(shape, lambda i: (0,) * len(shape))

    in_specs = [pl.BlockSpec((rows, D_MODEL), lambda i: (jnp.minimum(i, n_x_blk - 1), 0))]
    args = [x]
    if n_head:
        in_specs.append(const((n_head, D_MODEL)))
        args.append(head)
    in_specs += [const((1, D_MODEL)), const((1, D_MODEL)), const((D_MODEL, PROJ_COLS))]
    args += [ln_g, ln_b, w]
    out_shape = [jax.ShapeDtypeStruct((n, width), F32) for width in (CONV_DIM, GDN_VW, LANES, SB_W, SB_W)]
    out_specs = [row(width) for width in (CONV_DIM, GDN_VW, LANES, SB_W, SB_W)]
    scratch = []
    if n_head:
        out_shape += [jax.ShapeDtypeStruct((SB_W, length), F32),
                      jax.ShapeDtypeStruct((SB_W, length), F32),
                      jax.ShapeDtypeStruct((n_blk * sub, SB_W, BLK), BF16),
                      jax.ShapeDtypeStruct((n_blk * sub, BLK, SB_W), BF16),
                      jax.ShapeDtypeStruct((n_blk * sub, SB_W, BLK), BF16)]
        out_specs += [pl.BlockSpec((SB_W, rows), lambda i: (0, jnp.minimum(i, last_t))),
                      pl.BlockSpec((SB_W, rows), lambda i: (0, jnp.minimum(i, last_t))),
                      pl.BlockSpec((sub, SB_W, BLK), lambda i: (i, 0, 0)),
                      pl.BlockSpec((sub, BLK, SB_W), lambda i: (i, 0, 0)),
                      pl.BlockSpec((sub, SB_W, BLK), lambda i: (i, 0, 0))]
        scratch = [pltpu.VMEM((n_head, D_MODEL), F32), pltpu.VMEM((3, rows, SB_W), F32)]
    else:
        out_shape += [jax.ShapeDtypeStruct((n, SB_W), F32)] * 2
        out_specs += [row(SB_W), row(SB_W)]
    return pl.pallas_call(
        functools.partial(_project_kernel, n_head, q_scale),
        grid=(n_blk,),
        in_specs=in_specs,
        out_specs=out_specs,
        out_shape=out_shape,
        scratch_shapes=scratch,
        compiler_params=pltpu.CompilerParams(
            dimension_semantics=("arbitrary",), vmem_limit_bytes=VMEM_LIMIT),
        name="project",
    )(*args)


def _gdn_kernel(chunk, n_sub, hi, c_ref, ab_ref, cw_ref, alog_ref, dt_ref, s0_ref, conv0_ref,
                tri_ref, eye_ref, o_ref, s_ref, xs_ref):
    j = pl.program_id(1)
    C = chunk
    R = C * n_sub

    @pl.when(j == 0)
    def _():
        xs_ref[0:SUBLANES, :] = conv0_ref[0]
        s_ref[...] = s0_ref[...]

    valid = j * R + lax.broadcasted_iota(jnp.int32, (R, 1), 0) < hi

    xs_ref[SUBLANES:SUBLANES + R, :] = jnp.where(valid, c_ref[...], 0.0)
    first = SUBLANES - (CONV_W - 1)
    y = xs_ref[first:first + R, :] * cw_ref[0:1, :]
    for i in range(1, CONV_W):
        y = y + xs_ref[first + i:first + i + R, :] * cw_ref[i:i + 1, :]
    xs_ref[0:SUBLANES, :] = xs_ref[R:R + SUBLANES, :]
    y = y * _sigmoid(y)

    ab = ab_ref[...]
    g_all = jnp.where(valid, -jnp.exp(alog_ref[...]) * _softplus(ab + dt_ref[...]), 0.0)
    beta_all = jnp.where(valid, _sigmoid(ab), 0.0)
    ri = lax.broadcasted_iota(jnp.int32, (C, C), 0)
    ci = lax.broadcasted_iota(jnp.int32, (C, C), 1)
    causal = ri >= ci
    strict = ri > ci
    tri = tri_ref[...]
    eye = eye_ref[...]

    prep = []
    for cc in range(n_sub):
        rows = slice(cc * C, (cc + 1) * C)
        g = g_all[rows]
        beta = beta_all[rows]
        g_hi = g.astype(BF16)
        g_rest = g - g_hi.astype(F32)
        g_mid, g_lo = _split2(g_rest)
        gc = _dot(tri, g_hi) + (_dot(tri, g_mid) + _dot(tri, g_lo))
        gct = gc.T
        for h in range(GDN_HEADS):
            q = y[rows, h * GDN_DK:(h + 1) * GDN_DK]
            k = y[rows, GDN_QK + h * GDN_DK:GDN_QK + (h + 1) * GDN_DK]
            v = y[rows, 2 * GDN_QK + h * GDN_DV:2 * GDN_QK + (h + 1) * GDN_DV]
            q = q * lax.rsqrt(jnp.sum(q * q, axis=-1, keepdims=True) + L2_EPS) * (GDN_DK ** -0.5)
            k = k * lax.rsqrt(jnp.sum(k * k, axis=-1, keepdims=True) + L2_EPS)
            gcol = gc[:, h:h + 1]
            grow = gct[h:h + 1, :]
            bcol = beta[:, GDN_HEADS + h:GDN_HEADS + h + 1]
            glast = gc[C - 1:C, h:h + 1]
            decay = jnp.exp(jnp.where(causal, gcol - grow, -jnp.inf))
            kb = k * bcol
            eg = jnp.exp(gcol)
            prep.append(dict(
                m=jnp.where(strict, _dot1_nt(kb, k) * decay, 0.0),
                rhs=jnp.concatenate([v * bcol, kb * eg], axis=-1),
                qk=_dot1_nt(q, k) * decay, qd=q * eg,
                kdt=(k * jnp.exp(glast - gcol)).T,
                eg_last=jnp.exp(glast)))

    def dot3_split(a_split, b_split):
        (a_hi, a_lo), (b_hi, b_lo) = a_split, b_split
        return _dot(a_hi, b_hi) + (_dot(a_hi, b_lo) + _dot(a_lo, b_hi))

    ps = [-pr["m"] for pr in prep]
    ts = [eye + p for p in ps]
    p_splits = [_split2(p) for p in ps]
    for _ in range(int(math.log2(C)) - 1):
        p_splits = [_split2(dot3_split(sp, sp)) for sp in p_splits]
        ts = [t + dot3_split(_split2(t), sp) for t, sp in zip(ts, p_splits)]
    sols = [_dot3(t, pr["rhs"]) for t, pr in zip(ts, prep)]

    for cc in range(n_sub):
        ids = [cc * GDN_HEADS + h for h in range(GDN_HEADS)]
        s = [s_ref[0, h] for h in range(GDN_HEADS)]
        v_new = [sols[i][:, :GDN_DV] - _dot1(sols[i][:, GDN_DV:], s[h]) for h, i in enumerate(ids)]
        for h, i in enumerate(ids):
            s_ref[0, h] = s[h] * prep[i]["eg_last"] + _dot1(prep[i]["kdt"], v_new[h])
        for h, i in enumerate(ids):
            o_ref[cc * C:(cc + 1) * C, h * GDN_DV:(h + 1) * GDN_DV] = (
                _dot1(prep[i]["qd"], s[h]) + _dot1(prep[i]["qk"], v_new[h]))


def _gdn(c, ab, conv_w, a_log, dt_bias, s0, conv0, rows_per_seq, hi, n_sub):
    b = s0.shape[0]
    rows = GDN_CHUNK * n_sub
    n_steps = rows_per_seq // rows
    return pl.pallas_call(
        functools.partial(_gdn_kernel, GDN_CHUNK, n_sub, hi),
        grid=(b, n_steps),
        in_specs=[
            pl.BlockSpec((rows, CONV_DIM), lambda bi, j: (bi * n_steps + j, 0)),
            pl.BlockSpec((rows, LANES), lambda bi, j: (bi * n_steps + j, 0)),
            pl.BlockSpec((SUBLANES, CONV_DIM), lambda bi, j: (0, 0)),
            pl.BlockSpec((1, LANES), lambda bi, j: (0, 0)),
            pl.BlockSpec((1, LANES), lambda bi, j: (0, 0)),
            pl.BlockSpec((1, GDN_HEADS, GDN_DK, GDN_DV), lambda bi, j: (bi, 0, 0, 0)),
            pl.BlockSpec((1, SUBLANES, CONV_DIM), lambda bi, j: (bi, 0, 0)),
            pl.BlockSpec((GDN_CHUNK, GDN_CHUNK), lambda bi, j: (0, 0)),
            pl.BlockSpec((GDN_CHUNK, GDN_CHUNK), lambda bi, j: (0, 0)),
        ],
        out_specs=[
            pl.BlockSpec((rows, GDN_VW), lambda bi, j: (bi * n_steps + j, 0)),
            pl.BlockSpec((1, GDN_HEADS, GDN_DK, GDN_DV), lambda bi, j: (bi, 0, 0, 0)),
        ],
        out_shape=[
            jax.ShapeDtypeStruct((b * rows_per_seq, GDN_VW), F32),
            jax.ShapeDtypeStruct((b, GDN_HEADS, GDN_DK, GDN_DV), F32),
        ],
        scratch_shapes=[pltpu.VMEM((rows + SUBLANES, CONV_DIM), F32)],
        compiler_params=pltpu.CompilerParams(
            dimension_semantics=("arbitrary", "arbitrary"), vmem_limit_bytes=VMEM_LIMIT),
        name="gdn",
    )(c, ab, conv_w, a_log, dt_bias, s0, conv0,
      jnp.tril(jnp.ones((GDN_CHUNK, GDN_CHUNK), BF16)), jnp.eye(GDN_CHUNK, dtype=F32))


def _cumsum_weights():
    r = jnp.arange(BLK)[:, None]
    c = jnp.arange(2 * BLK)[None, :]
    return jnp.where(c < BLK, r >= c, True).astype(BF16)


def _softplus2(t2):
    neg_abs = lax.bitcast_convert_type(
        lax.bitcast_convert_type(t2, jnp.uint32) | jnp.uint32(0x80000000), F32)
    return jnp.maximum(t2, 0.0) + jnp.log(1.0 + jnp.exp2(neg_abs)) * LOG2E


def _block_sums(sp, tw):
    res = _dot(sp.astype(BF16), tw)
    return res[:, :BLK], res[:, BLK:]


def _sb_tile(t2, vis, carry, tw):
    sp = _softplus2(t2)
    if vis is not None:
        sp = jnp.where(vis, sp, 0.0)
    incl, total = _block_sums(sp, tw)
    a = jnp.exp2(t2 - incl - carry)
    if vis is not None:
        a = jnp.where(vis, a, 0.0)
    return a, carry + total


def _sb_prompt_kernel(bias_ref, qt_ref, kb_ref, vt_ref, u_ref, o_ref,
                      wq_ref, carry_ref, acc_ref, t2_ref, a_ref):
    i = pl.program_id(0)
    sub = qt_ref.shape[0]
    bq = sub * BLK
    pairs = range(SB_HEADS // 2)
    u = u_ref[...]
    kidx = lax.broadcasted_iota(jnp.int32, (BLK, bq), 0)
    qidx = i * bq + lax.broadcasted_iota(jnp.int32, (BLK, bq), 1)
    zeros_q = jnp.zeros((SB_HEAD_DIM, bq), BF16)
    zeros_k = jnp.zeros((SB_HEAD_DIM, BLK), BF16)

    def pair_cols(p):
        return slice(2 * SB_HEAD_DIM * p, 2 * SB_HEAD_DIM * (p + 1))

    for p in pairs:
        qtp = jnp.concatenate([qt_ref[s, pair_cols(p), :] for s in range(sub)], axis=1)
        wq_ref[p] = jnp.concatenate([
            jnp.concatenate([qtp[:SB_HEAD_DIM], zeros_q], axis=1),
            jnp.concatenate([zeros_q, qtp[SB_HEAD_DIM:]], axis=1)], axis=0)
    carry_ref[...] = jnp.zeros_like(carry_ref)
    acc_ref[...] = jnp.zeros_like(acc_ref)

    def logits(j, p):
        return _dot(kb_ref[j, :, pair_cols(p)], wq_ref[p])

    def block_sums(lp, p, vis):
        out = []
        for hh in range(2):
            t2 = lp[:, hh * bq:(hh + 1) * bq] + bias_ref[2 * p + hh]
            sp = _softplus2(t2)
            if vis is not None:
                sp = jnp.where(vis, sp, 0.0)
            res = _dot(u, sp.astype(BF16))
            out.append((t2, res[:BLK], res[BLK:BLK + SUBLANES]))
        return out

    def weights(p, staged, vis):
        a = []
        for hh, (t2, incl, total) in enumerate(staged):
            h = 2 * p + hh
            carry = carry_ref[h]
            e = (t2 - incl).reshape(BLK // SUBLANES, SUBLANES, bq) - carry[None]
            a_h = jnp.exp2(e).reshape(BLK, bq)
            if vis is not None:
                a_h = jnp.where(vis, a_h, 0.0)
            carry_ref[h] = carry + total
            a.append(a_h.astype(BF16))
        return jnp.concatenate(a, axis=0)

    def values(j, p, ap):
        vtp = vt_ref[j, pair_cols(p), :]
        vbdt = jnp.concatenate([
            jnp.concatenate([vtp[:SB_HEAD_DIM], zeros_k], axis=1),
            jnp.concatenate([zeros_k, vtp[SB_HEAD_DIM:]], axis=1)], axis=0)
        acc_ref[p] += _dot(vbdt, ap)

    for d in range(sub - 1, -1, -1):
        jd = i * sub + d
        vis = jd * BLK + kidx < qidx
        staged = [block_sums(logits(jd, p), p, vis) for p in pairs]
        for p in pairs:
            values(jd, p, weights(p, staged[p], vis))

    n = i * sub
    for p in pairs:
        t2_ref[p] = logits(jnp.maximum(n - 1, 0), p)
    a_ref[...] = jnp.zeros_like(a_ref)

    def trip(j):
        j_next = jnp.maximum(j - 1, 0)
        staged = []
        for p in pairs:
            lp = t2_ref[p]
            t2_ref[p] = logits(j_next, p)
            staged.append(block_sums(lp, p, None))
            values(j + 1, p, a_ref[p])
        for p in pairs:
            a_ref[p] = weights(p, staged[p], None)

    def body(jj, _):
        for d in range(2 * sub):
            trip(n - 1 - (jj * 2 * sub + d))
        return 0

    lax.fori_loop(0, i // 2, body, 0)

    @pl.when(i % 2 == 1)
    def _():
        for d in range(sub - 1, -1, -1):
            trip(d)
    for p in pairs:
        values(0, p, a_ref[p])
        o_ref[:, pair_cols(p)] = acc_ref[p].T


def _sb_prompt(qt, kb, vt, bias2, u, bq):
    n_kblk = kb.shape[0]
    sub = bq // BLK
    assert n_kblk % sub == 0
    return pl.pallas_call(
        _sb_prompt_kernel,
        grid=(n_kblk // sub,),
        in_specs=[
            pl.BlockSpec(memory_space=pltpu.SMEM),
            pl.BlockSpec((sub, SB_W, BLK), lambda i: (i, 0, 0)),
            pl.BlockSpec((n_kblk, BLK, SB_W), lambda i: (0, 0, 0), pipeline_mode=pl.Buffered(1)),
            pl.BlockSpec((n_kblk, SB_W, BLK), lambda i: (0, 0, 0), pipeline_mode=pl.Buffered(1)),
            pl.BlockSpec((BLK + 2 * SUBLANES, BLK), lambda i: (0, 0)),
        ],
        out_specs=pl.BlockSpec((bq, SB_W), lambda i: (i, 0)),
        out_shape=jax.ShapeDtypeStruct((n_kblk * BLK, SB_W), F32),
        scratch_shapes=[pltpu.VMEM((SB_HEADS // 2, 2 * SB_HEAD_DIM, 2 * bq), BF16),
                        pltpu.VMEM((SB_HEADS, SUBLANES, bq), F32),
                        pltpu.VMEM((SB_HEADS // 2, 2 * SB_HEAD_DIM, bq), F32),
                        pltpu.VMEM((SB_HEADS // 2, BLK, 2 * bq), F32),
                        pltpu.VMEM((SB_HEADS // 2, 2 * BLK, bq), BF16)],
        compiler_params=pltpu.CompilerParams(
            dimension_semantics=("arbitrary",), vmem_limit_bytes=VMEM_LIMIT),
        name="sb_prompt",
    )(bias2, qt, kb, vt, u)


def _sb_decode_kernel(t_new, pages_per_step, pt_ref, q_ref, kn_ref, vn_ref, bias_ref, tw_ref, *refs):
    k_refs = refs[:pages_per_step]
    v_refs = refs[pages_per_step:2 * pages_per_step]
    o_ref, qbd_ref, carry_ref, acc_ref = refs[2 * pages_per_step:]
    j = pl.program_id(1)
    rows = SB_HEADS * t_new
    tw = tw_ref[...]
    bias = bias_ref[...]
    head_of_lane = lax.broadcasted_iota(jnp.int32, (t_new, SB_W), 1) // SB_HEAD_DIM

    @pl.when(j == 0)
    def _():
        q = q_ref[0]
        qbd_ref[...] = jnp.concatenate(
            [jnp.where(head_of_lane == h, q, 0.0) for h in range(SB_HEADS)], axis=0).astype(BF16)
        pad_rows = jnp.zeros((BLK - t_new, SB_W), F32)
        knt = jnp.concatenate([kn_ref[0], pad_rows], axis=0).T.astype(BF16)
        vnt = jnp.concatenate([vn_ref[0], pad_rows], axis=0).T.astype(BF16)
        tq = lax.broadcasted_iota(jnp.int32, (rows, BLK), 0) % t_new
        ks = lax.broadcasted_iota(jnp.int32, (rows, BLK), 1)
        t2 = _dot(qbd_ref[...], knt) + bias
        a, c = _sb_tile(t2, ks < tq, jnp.zeros((rows, BLK), F32), tw)
        carry_ref[...] = c
        acc_ref[...] = _dot_nt(a.astype(BF16), vnt)

    qbd = qbd_ref[...]
    t2s = [_dot(qbd, k_refs[r][0].astype(BF16)) + bias for r in range(pages_per_step)]
    sps = [_softplus2(t2) for t2 in t2s]
    incl, total = _block_sums(jnp.concatenate(sps, axis=0), tw)
    carry = carry_ref[...]
    acc = acc_ref[...]
    for r in range(pages_per_step):
        rs = slice(r * rows, (r + 1) * rows)
        a = jnp.exp2(t2s[r] - incl[rs] - carry)
        carry = carry + total[rs]
        acc = acc + _dot_nt(a.astype(BF16), v_refs[r][0].astype(BF16))
    carry_ref[...] = carry
    acc_ref[...] = acc

    @pl.when(j == pl.num_programs(1) - 1)
    def _():
        out = jnp.zeros((t_new, SB_W), F32)
        for h in range(SB_HEADS):
            out = out + jnp.where(head_of_lane == h, acc[h * t_new:(h + 1) * t_new], 0.0)
        o_ref[0] = out


def _sb_decode(q, k_new, v_new, cache_kt, cache_vt, page_table, bias_rows, tw, pages_per_step):
    b, t_new, _ = q.shape
    n_pages = page_table.shape[1]
    page = cache_kt.shape[2]
    assert page == BLK and n_pages % pages_per_step == 0
    n_steps = n_pages // pages_per_step
    rows = SB_HEADS * t_new

    def page_spec(r):
        return pl.BlockSpec(
            (1, SB_W, page),
            lambda bi, j, pt: (pt[bi, n_pages - 1 - (j * pages_per_step + r)], 0, 0))

    seq = pl.BlockSpec((1, t_new, SB_W), lambda bi, j, pt: (bi, 0, 0))
    grid_spec = pltpu.PrefetchScalarGridSpec(
        num_scalar_prefetch=1,
        grid=(b, n_steps),
        in_specs=[seq, seq, seq,
                  pl.BlockSpec((rows, BLK), lambda bi, j, pt: (0, 0)),
                  pl.BlockSpec((BLK, 2 * BLK), lambda bi, j, pt: (0, 0))]
        + [page_spec(r) for r in range(pages_per_step)]
        + [page_spec(r) for r in range(pages_per_step)],
        out_specs=seq,
        scratch_shapes=[pltpu.VMEM((rows, SB_W), BF16),
                        pltpu.VMEM((rows, BLK), F32),
                        pltpu.VMEM((rows, SB_W), F32)],
    )
    return pl.pallas_call(
        functools.partial(_sb_decode_kernel, t_new, pages_per_step),
        grid_spec=grid_spec,
        out_shape=jax.ShapeDtypeStruct((b, t_new, SB_W), F32),
        compiler_params=pltpu.CompilerParams(
            dimension_semantics=("arbitrary", "arbitrary"), vmem_limit_bytes=VMEM_LIMIT),
        name="sb_decode",
    )(page_table, q, k_new, v_new, bias_rows, tw,
      *([cache_kt] * pages_per_step), *([cache_vt] * pages_per_step))


def _merge_kernel(alpha, shift, *refs):
    x_ref = refs[0]
    n_mix = 4
    if shift:
        mixer = [jnp.concatenate([refs[1 + 2 * m][shift:, :], refs[2 + 2 * m][...]], axis=0)
                 for m in range(n_mix)]
        refs = refs[1 + 2 * n_mix:]
    else:
        mixer = [refs[1 + m][...] for m in range(n_mix)]
        refs = refs[1 + n_mix:]
    og, zg, osb, zs = mixer
    lig_ref, lib_ref, ng_ref, w_ref, log_ref, lob_ref, y_ref = refs
    h = _layernorm(x_ref[...], lig_ref[...], lib_ref[...])
    parts = []
    for hd in range(GDN_HEADS):
        sl = slice(hd * GDN_DV, (hd + 1) * GDN_DV)
        o = og[:, sl]
        o = o * lax.rsqrt(jnp.mean(o * o, axis=-1, keepdims=True) + RMS_EPS) * ng_ref[...]
        z = zg[:, sl]
        parts.append(o * (z * _sigmoid(z)))
    parts.append(osb * (zs * _sigmoid(zs)))
    cat = jnp.concatenate(parts, axis=-1).astype(BF16)
    mix = _dot(cat, w_ref[...])
    y_ref[...] = _layernorm(alpha * h + mix, log_ref[...], lob_ref[...])


def _merge(x, mixer, ln_in_g, ln_in_b, norm_g, w_out, ln_out_g, ln_out_b, shift, alpha):
    n = x.shape[0]
    rows = MERGE_ROWS
    n_blk = n // rows
    assert n % rows == 0 and (rows % shift == 0 if shift else True)
    vec = pl.BlockSpec((1, D_MODEL), lambda i: (0, 0))
    in_specs = [pl.BlockSpec((rows, D_MODEL), lambda i: (i, 0))]
    args = [x]
    for m in mixer:
        in_specs.append(pl.BlockSpec((rows, m.shape[1]), lambda i: (i, 0)))
        args.append(m)
        if shift:
            per_blk = rows // shift
            in_specs.append(pl.BlockSpec((shift, m.shape[1]), lambda i: ((i + 1) * per_blk, 0)))
            args.append(m)
    in_specs += [vec, vec, pl.BlockSpec((1, GDN_DV), lambda i: (0, 0)),
                 pl.BlockSpec((D_MODEL, D_MODEL), lambda i: (0, 0)), vec, vec]
    args += [ln_in_g, ln_in_b, norm_g, w_out, ln_out_g, ln_out_b]
    return pl.pallas_call(
        functools.partial(_merge_kernel, alpha, shift),
        grid=(n_blk,),
        in_specs=in_specs,
        out_specs=pl.BlockSpec((rows, D_MODEL), lambda i: (i, 0)),
        out_shape=jax.ShapeDtypeStruct((n, D_MODEL), F32),
        compiler_params=pltpu.CompilerParams(
            dimension_semantics=("arbitrary",), vmem_limit_bytes=VMEM_LIMIT),
        name="merge",
    )(*args)


def _reorder_w_in(w_in):
    offs = [0]
    for s in PROJ_SIZES:
        offs.append(offs[-1] + s)
    c, zg, a, b, q, k, v, zs = [w_in[:, offs[i]:offs[i + 1]] for i in range(len(PROJ_SIZES))]
    pad = jnp.zeros((w_in.shape[0], LANES - 2 * GDN_HEADS), w_in.dtype)
    return jnp.concatenate([c, zg, q, k, v, zs, a, b, pad], axis=1).astype(BF16)


def _lane_row(v, width=LANES):
    return jnp.pad(v.astype(F32), (0, width - v.shape[0])).reshape(1, width)


def kernel(x_prompt, x_sample, cache_k, cache_v, page_table, state_gdn, state_conv, meta_tokens,
           ln_in_g, ln_in_b, w_in, conv_w, a_log, dt_bias, gdn_norm_g, sb_bias, w_out, ln_out_g,
           ln_out_b):
    depth = w_in.shape[0]
    assert depth == 1, "single-layer trunk only"
    batch, seq, _ = x_prompt.shape
    dec_batch, dec_seq, _ = x_sample.shape
    assert batch == 1 and seq % BLK == 0 and meta_tokens.shape[0] == N_META
    assert (dec_batch * dec_seq) % BLK == 0 and dec_seq <= GDN_CHUNK
    alpha = (2 * depth) ** 0.25
    length = N_META + seq
    n_pad = -(-length // SB_BQ) * SB_BQ
    q_scale = LOG2E * SB_HEAD_DIM ** -0.5

    lig = ln_in_g.reshape(1, D_MODEL)
    lib = ln_in_b.reshape(1, D_MODEL)
    w = _reorder_w_in(w_in[0])
    w_o = w_out[0].astype(BF16)
    cw = jnp.pad(conv_w[0], ((0, SUBLANES - CONV_W), (0, 0)))
    alog = _lane_row(a_log[0])
    dtb = _lane_row(dt_bias[0])
    norm_g = gdn_norm_g[0].reshape(1, GDN_DV)
    log_ = ln_out_g[0].reshape(1, D_MODEL)
    lob = ln_out_b[0].reshape(1, D_MODEL)
    tw = _cumsum_weights()
    bias = sb_bias[0].astype(F32) * LOG2E

    xp = x_prompt[0]
    c_p, zg_p, ab_p, _, zs_p, kt32_p, vt32_p, qt_p, kb_p, vt_p = _project(
        xp, meta_tokens.astype(F32), lig, lib, w, q_scale, n_pad // PROJ_ROWS)
    og_p, s_p = _gdn(c_p, ab_p, cw, alog, dtb,
                     jnp.zeros((1, GDN_HEADS, GDN_DK, GDN_DV), F32),
                     jnp.zeros((1, SUBLANES, CONV_DIM), F32), n_pad, length, GDN_CHUNKS_PER_STEP)
    idx = jnp.arange(BLK)
    u = jnp.concatenate([idx[None, :] >= idx[:, None], jnp.ones((2 * SUBLANES, BLK), bool)]).astype(BF16)
    os_p = _sb_prompt(qt_p, kb_p, vt_p, bias, u, SB_BQ)
    y_p = _merge(xp, (og_p, zg_p, os_p, zs_p), lig, lib, norm_g, w_o, log_, lob, N_META, alpha)

    xs = x_sample.reshape(dec_batch * dec_seq, D_MODEL)
    c_s, zg_s, ab_s, q_s, zs_s, k_s, v_s = _project(xs, None, lig, lib, w, q_scale,
                                                     dec_batch * dec_seq // PROJ_ROWS)

    def per_seq_chunk(a):
        a = a.reshape(dec_batch, dec_seq, a.shape[-1])
        a = jnp.pad(a, ((0, 0), (0, GDN_CHUNK - dec_seq), (0, 0)))
        return a.reshape(dec_batch * GDN_CHUNK, a.shape[-1])

    conv0 = jnp.pad(state_conv[0].astype(F32), ((0, 0), (SUBLANES - (CONV_W - 1), 0), (0, 0)))
    og_s, s_s = _gdn(per_seq_chunk(c_s), per_seq_chunk(ab_s), cw, alog, dtb,
                     state_gdn[0].astype(F32), conv0, GDN_CHUNK, dec_seq, 1)
    og_s = og_s.reshape(dec_batch, GDN_CHUNK, GDN_VW)[:, :dec_seq].reshape(dec_batch * dec_seq, GDN_VW)
    n_phys, page = cache_k.shape[1], cache_k.shape[2]

    def pages_t(cache):
        return jnp.transpose(cache, (0, 2, 3, 1)).reshape(n_phys, SB_W, page)

    os_s = _sb_decode(q_s.reshape(dec_batch, dec_seq, SB_W), k_s.reshape(dec_batch, dec_seq, SB_W),
                      v_s.reshape(dec_batch, dec_seq, SB_W),
                      pages_t(cache_k[0]), pages_t(cache_v[0]), page_table,
                      jnp.broadcast_to(jnp.repeat(bias, dec_seq)[:, None], (SB_HEADS * dec_seq, BLK)),
                      tw, DECODE_PAGES_PER_STEP)
    y_s = _merge(xs, (og_s, zg_s, os_s.reshape(dec_batch * dec_seq, SB_W), zs_s), lig, lib, norm_g,
                 w_o, log_, lob, 0, alpha)

    def heads_last(t):
        return jnp.transpose(t.reshape(SB_HEADS, SB_HEAD_DIM, length), (2, 0, 1))[None, None]

    kv_s = (1, dec_batch, dec_seq, SB_HEADS, SB_HEAD_DIM)
    c_s3 = c_s.reshape(dec_batch, dec_seq, CONV_DIM)
    conv_s = jnp.concatenate([state_conv[0].astype(F32), c_s3], axis=1)[:, -(CONV_W - 1):]
    return (y_p[None], y_s.reshape(dec_batch, dec_seq, D_MODEL),
            heads_last(kt32_p), heads_last(vt32_p),
            k_s.reshape(kv_s), v_s.reshape(kv_s),
            s_p[None], s_s[None],
            c_p[length - (CONV_W - 1):length].reshape(1, 1, CONV_W - 1, CONV_DIM), conv_s[None])
```
